```python
import math
import jax
import jax.numpy as jnp
from jax import lax
import numpy as np


D_MODEL = 1024
BATCH = 8
SEQ = 4096
DEPTH = 2

EPS = 1e-6
CONV_WIDTH = 4
MIX_WIDTH = 2 * D_MODEL
ATT_HEAD_DIM = 64
ATT_WIDTH = MIX_WIDTH // 4
ATT_HEADS = ATT_WIDTH // ATT_HEAD_DIM
Q_BLOCK = 128
SSD_HEAD_DIM = 64
SSD_WIDTH = MIX_WIDTH // 2
SSD_HEADS = SSD_WIDTH // SSD_HEAD_DIM
SSD_GROUPS = 2
SSD_STATE = 128
SSD_CHUNK = 128
SSD_CONV_DIM = SSD_WIDTH + 2 * SSD_GROUPS * SSD_STATE
DT_MIN = 0.001
DT_MAX = 0.1
LRU_WIDTH = MIX_WIDTH // 4
LRU_BLOCKS = 8
LRU_BLOCK_DIM = LRU_WIDTH // LRU_BLOCKS
LRU_C = 8.0
IN_SIZES = (ATT_WIDTH, ATT_WIDTH, ATT_WIDTH, ATT_WIDTH, ATT_HEADS,
            SSD_WIDTH, SSD_CONV_DIM, SSD_HEADS,
            LRU_WIDTH, LRU_WIDTH)
D_IN = sum(IN_SIZES)

kernel_name = 'hymba_fox_ssd_rglru_hybrid'


def rms_norm(x, g):
    xf = x.astype(jnp.float32)
    return xf * lax.rsqrt(jnp.mean(xf * xf, axis=-1, keepdims=True) + EPS) * g


def causal_dwconv(x, w, b):
    k = w.shape[0]
    y = lax.conv_general_dilated(
        x, w.astype(x.dtype)[:, None, :], window_strides=(1,),
        padding=((k - 1, 0),), dimension_numbers=('NWC', 'WIO', 'NWC'),
        feature_group_count=x.shape[-1])
    return y + b


def fox_attention(q, k, v, log_f, q_g, k_g):
    bsz, seq = q.shape[0], q.shape[1]
    nb = seq // Q_BLOCK
    q = rms_norm(q, q_g)
    k = rms_norm(k, k_g)
    c = jnp.cumsum(log_f.astype(jnp.float32), axis=1)
    qb = q.reshape(bsz, nb, Q_BLOCK, ATT_HEADS, ATT_HEAD_DIM).transpose(1, 0, 3, 2, 4)
    cq = c.reshape(bsz, nb, Q_BLOCK, ATT_HEADS).transpose(1, 0, 3, 2)
    kt = k.transpose(0, 2, 1, 3)
    vt = v.transpose(0, 2, 1, 3)
    ck = c.transpose(0, 2, 1)
    kpos = jnp.arange(seq)
    scale = ATT_HEAD_DIM ** -0.5

    def one_block(args):
        q_i, c_i, blk = args
        logits = jnp.einsum('bhqd,bhkd->bhqk', q_i, kt) * scale + (c_i[..., :, None] - ck[:, :, None, :])
        qpos = blk * Q_BLOCK + jnp.arange(Q_BLOCK)
        logits = jnp.where(kpos[None, :] <= qpos[:, None], logits, -jnp.inf)
        p = jax.nn.softmax(logits.astype(jnp.float32), axis=-1)
        return jnp.einsum('bhqk,bhkd->bhqd', p, vt)

    o = lax.map(one_block, (qb, cq, jnp.arange(nb)))
    return o.transpose(1, 0, 3, 2, 4).reshape(bsz, seq, ATT_WIDTH)


def ssd_mixer(xbc_raw, dt_raw, z, conv_w, conv_b, dt_bias, a_log, d_skip, norm_g):
    bsz, seq = xbc_raw.shape[0], xbc_raw.shape[1]
    nc = seq // SSD_CHUNK
    e = SSD_HEADS // SSD_GROUPS
    gn = SSD_GROUPS * SSD_STATE
    xbc = jax.nn.silu(causal_dwconv(xbc_raw, conv_w, conv_b))
    xs, bm, cm = jnp.split(xbc, [SSD_WIDTH, SSD_WIDTH + gn], axis=-1)
    dt = jax.nn.softplus(dt_raw + dt_bias)
    a = -jnp.exp(a_log.astype(jnp.float32))
    xh = xs.reshape(bsz, seq, SSD_HEADS, SSD_HEAD_DIM)
    xdt = (xh * dt[..., None]).reshape(bsz, nc, SSD_CHUNK, SSD_GROUPS, e, SSD_HEAD_DIM)
    adt = (dt * a).reshape(bsz, nc, SSD_CHUNK, SSD_GROUPS, e).transpose(0, 1, 3, 4, 2)
    bc = bm.reshape(bsz, nc, SSD_CHUNK, SSD_GROUPS, SSD_STATE)
    cc = cm.reshape(bsz, nc, SSD_CHUNK, SSD_GROUPS, SSD_STATE)
    a_cs = jnp.cumsum(adt, axis=-1)
    causal = jnp.tril(jnp.ones((SSD_CHUNK, SSD_CHUNK), dtype=bool))
    seg = a_cs[..., :, None] - a_cs[..., None, :]
    lmat = jnp.exp(jnp.where(causal, seg, -jnp.inf))
    cb = jnp.einsum('bclgn,bcsgn->bcgls', cc, bc)
    y_diag = jnp.einsum('bcgels,bcsgep->bclgep', cb[:, :, :, None] * lmat, xdt)
    decay_states = jnp.exp(a_cs[..., -1:] - a_cs)
    states = jnp.einsum('bclgn,bcgel,bclgep->bcgepn', bc, decay_states, xdt)
    tot = jnp.concatenate([jnp.zeros_like(a_cs[:, :1, ..., -1]), a_cs[..., -1]], axis=1)
    tot_cs = jnp.cumsum(tot, axis=1)
    seg_c = tot_cs[:, :, None] - tot_cs[:, None, :]
    causal_c = jnp.tril(jnp.ones((nc + 1, nc + 1), dtype=bool))[None, :, :, None, None]
    decay_chunk = jnp.exp(jnp.where(causal_c, seg_c, -jnp.inf))
    states_pad = jnp.concatenate([jnp.zeros_like(states[:, :1]), states], axis=1)
    new_states = jnp.einsum('bzcge,bcgepn->bzgepn', decay_chunk, states_pad)
    prev_states = new_states[:, :nc]
    y_off = jnp.einsum('bclgn,bcgepn,bcgel->bclgep', cc, prev_states, jnp.exp(a_cs))
    y = (y_diag + y_off).reshape(bsz, seq, SSD_HEADS, SSD_HEAD_DIM) + xh * d_skip[:, None]
    y = y.reshape(bsz, seq, SSD_WIDTH)
    gated = (y * jax.nn.silu(z)).reshape(bsz, seq, SSD_GROUPS, SSD_WIDTH // SSD_GROUPS)
    gated = rms_norm(gated, 1.0).reshape(bsz, seq, SSD_WIDTH)
    return gated * norm_g


def _lin_combine(c1, c2):
    a1, b1 = c1
    a2, b2 = c2
    return a1 * a2, a2 * b1 + b2


def rglru_mixer(x_raw, conv_w, conv_b, w_a, b_a, w_x, b_x, lam):
    bsz, seq = x_raw.shape[0], x_raw.shape[1]
    xc = causal_dwconv(x_raw, conv_w, conv_b)
    xb = xc.reshape(bsz, seq, LRU_BLOCKS, LRU_BLOCK_DIM)
    r = jax.nn.sigmoid(jnp.einsum('bsnd,nde->bsne', xb, w_a).reshape(bsz, seq, LRU_WIDTH) + b_a)
    i = jax.nn.sigmoid(jnp.einsum('bsnd,nde->bsne', xb, w_x).reshape(bsz, seq, LRU_WIDTH) + b_x)
    log_a = (-LRU_C * r * jax.nn.softplus(-lam)).astype(jnp.float32)
    a = jnp.exp(log_a)
    mult = jnp.sqrt(-jnp.expm1(2.0 * log_a))
    mult = jnp.where((jnp.arange(seq) == 0)[None, :, None], 1.0, mult)
    b = mult * (i * xc)
    _, h = lax.associative_scan(_lin_combine, (a, b), axis=1)
    return h


def setup_inputs(seed: int = 0) -> dict:
    key = jax.random.key(seed)
    ks = jax.random.split(key, 24)
    f32 = jnp.float32
    L = DEPTH
    nrm = lambda k, s: jax.random.normal(k, s, f32)
    x = nrm(ks[0], (BATCH, SEQ, D_MODEL))
    norm_g = 1.0 + 0.05 * nrm(ks[1], (L, D_MODEL))
    w_in = nrm(ks[2], (L, D_MODEL, D_IN)) * D_MODEL ** -0.5
    q_norm_g = 1.0 + 0.05 * nrm(ks[3], (L, ATT_HEAD_DIM))
    k_norm_g = 1.0 + 0.05 * nrm(ks[4], (L, ATT_HEAD_DIM))
    forget_b = jax.random.uniform(ks[5], (L, ATT_HEADS), f32, minval=1.0, maxval=4.0)
    ssd_conv_w = nrm(ks[6], (L, CONV_WIDTH, SSD_CONV_DIM)) * CONV_WIDTH ** -0.5
    ssd_conv_b = 0.02 * nrm(ks[7], (L, SSD_CONV_DIM))
    dt0 = jnp.exp(jax.random.uniform(ks[8], (L, SSD_HEADS), f32) * (math.log(DT_MAX) - math.log(DT_MIN)) + math.log(DT_MIN))
    ssd_dt_bias = dt0 + jnp.log(-jnp.expm1(-dt0))
    ssd_a_log = jnp.log(jax.random.uniform(ks[9], (L, SSD_HEADS), f32, minval=1.0, maxval=16.0))
    ssd_d = 1.0 + 0.1 * nrm(ks[10], (L, SSD_HEADS))
    ssd_norm_g = 1.0 + 0.05 * nrm(ks[11], (L, SSD_WIDTH))
    lru_conv_w = nrm(ks[12], (L, CONV_WIDTH, LRU_WIDTH)) * CONV_WIDTH ** -0.5
    lru_conv_b = 0.02 * nrm(ks[13], (L, LRU_WIDTH))
    lru_w_a = nrm(ks[14], (L, LRU_BLOCKS, LRU_BLOCK_DIM, LRU_BLOCK_DIM)) * LRU_BLOCK_DIM ** -0.5
    lru_b_a = 0.02 * nrm(ks[15], (L, LRU_WIDTH))
    lru_w_x = nrm(ks[16], (L, LRU_BLOCKS, LRU_BLOCK_DIM, LRU_BLOCK_DIM)) * LRU_BLOCK_DIM ** -0.5
    lru_b_x = 0.02 * nrm(ks[17], (L, LRU_WIDTH))
    a_pow_c = jax.random.uniform(ks[18], (L, LRU_WIDTH), f32, minval=0.9, maxval=0.999)
    s = a_pow_c ** (1.0 / LRU_C)
    lru_lambda = jnp.log(s) - jnp.log1p(-s)
    w_out = nrm(ks[19], (L, MIX_WIDTH, D_MODEL)) * MIX_WIDTH ** -0.5
    return {'x': x, 'norm_g': norm_g, 'w_in': w_in, 'q_norm_g': q_norm_g, 'k_norm_g': k_norm_g,
            'forget_b': forget_b, 'ssd_conv_w': ssd_conv_w, 'ssd_conv_b': ssd_conv_b,
            'ssd_dt_bias': ssd_dt_bias, 'ssd_a_log': ssd_a_log, 'ssd_d': ssd_d,
            'ssd_norm_g': ssd_norm_g, 'lru_conv_w': lru_conv_w, 'lru_conv_b': lru_conv_b,
            'lru_w_a': lru_w_a, 'lru_b_a': lru_b_a, 'lru_w_x': lru_w_x, 'lru_b_x': lru_b_x,
            'lru_lambda': lru_lambda, 'w_out': w_out}


def reference(x, norm_g, w_in, q_norm_g, k_norm_g, forget_b, ssd_conv_w, ssd_conv_b,
              ssd_dt_bias, ssd_a_log, ssd_d, ssd_norm_g, lru_conv_w, lru_conv_b,
              lru_w_a, lru_b_a, lru_w_x, lru_b_x, lru_lambda, w_out):
    out_dtype = x.dtype
    h = x.astype(jnp.float32)
    bsz, seq = h.shape[0], h.shape[1]
    split_points = np.cumsum(IN_SIZES)[:-1].tolist()
    for l in range(DEPTH):
        u = rms_norm(h, norm_g[l])
        proj = jnp.einsum('bsd,de->bse', u, w_in[l])
        (q, k, v, z_att, f_raw, z_ssd, xbc, dt_raw, x_lru, z_lru) = jnp.split(proj, split_points, axis=-1)
        hs = (bsz, seq, ATT_HEADS, ATT_HEAD_DIM)
        log_f = jax.nn.log_sigmoid(f_raw + forget_b[l])
        y_att = fox_attention(q.reshape(hs), k.reshape(hs), v.reshape(hs), log_f,
                              q_norm_g[l], k_norm_g[l]) * jax.nn.silu(z_att)
        y_ssd = ssd_mixer(xbc, dt_raw, z_ssd, ssd_conv_w[l], ssd_conv_b[l], ssd_dt_bias[l],
                          ssd_a_log[l], ssd_d[l], ssd_norm_g[l])
        y_lru = rglru_mixer(x_lru, lru_conv_w[l], lru_conv_b[l], lru_w_a[l], lru_b_a[l],
                            lru_w_x[l], lru_b_x[l], lru_lambda[l]) * jax.nn.silu(z_lru)
        mix = jnp.concatenate([y_att, y_ssd, y_lru], axis=-1)
        h = h + jnp.einsum('bse,ed->bsd', mix, w_out[l])
    return h.astype(out_dtype)
```

```python
import functools
import math

import jax
import jax.numpy as jnp
from jax import lax
from jax.experimental import pallas as pl
from jax.experimental.pallas import tpu as pltpu

F32 = jnp.float32
BF16 = jnp.bfloat16

V7X_LANES = 128
V7X_SUBLANES = 8
V7X_VMEM_BYTES = 64 * 1024 * 1024

D_MODEL = 1024
EPS = 1e-6
CONV_WIDTH = 4
MIX_WIDTH = 2 * D_MODEL
ATT_HEAD_DIM = 64
ATT_WIDTH = MIX_WIDTH // 4
ATT_HEADS = ATT_WIDTH // ATT_HEAD_DIM
SSD_HEAD_DIM = 64
SSD_WIDTH = MIX_WIDTH // 2
SSD_HEADS = SSD_WIDTH // SSD_HEAD_DIM
SSD_GROUPS = 2
SSD_STATE = 128
SSD_CHUNK = 128
SSD_CONV_DIM = SSD_WIDTH + 2 * SSD_GROUPS * SSD_STATE
LRU_WIDTH = MIX_WIDTH // 4
LRU_BLOCKS = 8
LRU_C = 8.0
IN_SIZES = (ATT_WIDTH, ATT_WIDTH, ATT_WIDTH, ATT_WIDTH, ATT_HEADS,
            SSD_WIDTH, SSD_CONV_DIM, SSD_HEADS, LRU_WIDTH, LRU_WIDTH)

LOG2E = 1.4426950408889634
HALO = V7X_SUBLANES

_SEGS = (("q", ATT_WIDTH), ("k", ATT_WIDTH), ("v", ATT_WIDTH), ("z_att", ATT_WIDTH),
         ("z_ssd", SSD_WIDTH), ("xbc", SSD_CONV_DIM), ("x_lru", LRU_WIDTH), ("z_lru", LRU_WIDTH),
         ("f_raw", V7X_LANES), ("dt_raw", V7X_LANES))
_SEG_BOUNDS = tuple((sum(w for _, w in _SEGS[:i]), sum(w for _, w in _SEGS[:i + 1]))
                    for i in range(len(_SEGS)))
D_IN_PAD = _SEG_BOUNDS[-1][1]

TM_INPROJ = 256
TS_PREP = 256
TQ_ATT = 512
TK_ATT = 512
SSD_CHUNKS_PER_STEP = 2
TS_LRU = 512
TM_OUTPROJ = 512


def _vmem_limit(nbytes):
    return int(min(nbytes * 3 // 2 + (8 << 20), V7X_VMEM_BYTES - (6 << 20)))


def _params(semantics, nbytes):
    return pltpu.CompilerParams(dimension_semantics=semantics, vmem_limit_bytes=_vmem_limit(nbytes))


def _sigmoid(x):
    return 1.0 / (1.0 + jnp.exp(-x))


def _silu(x):
    return x * _sigmoid(x)


def _softplus(x):
    return jnp.maximum(x, 0.0) + jnp.log1p(jnp.exp(-jnp.abs(x)))


def _split3(x):
    hi = x.astype(BF16)
    r1 = x - hi.astype(F32)
    mid = r1.astype(BF16)
    lo = (r1 - mid.astype(F32)).astype(BF16)
    return hi, mid, lo


def _tril_ones(n):
    row = lax.broadcasted_iota(jnp.int32, (n, n), 0)
    col = lax.broadcasted_iota(jnp.int32, (n, n), 1)
    return jnp.where(col <= row, 1.0, 0.0).astype(BF16)


def _cumsum_rows(x, tri):
    hi, mid, lo = _split3(x)
    dot = lambda a: jnp.dot(tri, a, preferred_element_type=F32)
    return dot(hi) + dot(mid) + dot(lo)


def _inproj_kernel(x_ref, g_ref, w_ref, *out_refs):
    x = x_ref[...]
    ms = jnp.mean(x * x, axis=-1, keepdims=True)
    u = (x * lax.rsqrt(ms + EPS) * g_ref[...]).astype(BF16)
    for o_ref, (lo, hi) in zip(out_refs, _SEG_BOUNDS):
        o_ref[...] = jnp.dot(u, w_ref[:, lo:hi], preferred_element_type=F32)


def _inproj(h, g, w):
    t = h.shape[0]
    tm = TM_INPROJ
    widths = [hi - lo for lo, hi in _SEG_BOUNDS]
    nbytes = 2 * (tm * D_MODEL * 4 + D_MODEL * D_IN_PAD * 2 + tm * D_IN_PAD * 4)
    return pl.pallas_call(
        _inproj_kernel,
        grid=(t // tm,),
        in_specs=[pl.BlockSpec((tm, D_MODEL), lambda i: (i, 0)),
                  pl.BlockSpec((1, D_MODEL), lambda i: (0, 0)),
                  pl.BlockSpec((D_MODEL, D_IN_PAD), lambda i: (0, 0))],
        out_specs=[pl.BlockSpec((tm, wd), lambda i: (i, 0)) for wd in widths],
        out_shape=[jax.ShapeDtypeStruct((t, wd), F32) for wd in widths],
        compiler_params=_params(("parallel",), nbytes),
        name="inproj",
    )(h, g, w)


def _att_prep_kernel(q_ref, k_ref, v_ref, f_ref, qg_ref, kg_ref, fb_ref,
                     qa_ref, ka_ref, va_ref, carry_ref):
    ts = q_ref.shape[1]

    @pl.when(pl.program_id(1) == 0)
    def _():
        carry_ref[...] = jnp.zeros_like(carry_ref)

    lane = lax.broadcasted_iota(jnp.int32, (ts, V7X_LANES), 1)
    lo_half = lane < ATT_HEAD_DIM

    fl = f_ref[0] + fb_ref[...]
    log_f = -_softplus(-fl)
    c = carry_ref[...] + _cumsum_rows(log_f, _tril_ones(ts))
    carry_ref[...] = c[ts - 1:ts, :]
    b_hi, b_mid, b_lo = (a.astype(F32) for a in _split3(c * (-LOG2E)))

    ones3 = jnp.where((lane >= ATT_HEAD_DIM) & (lane < ATT_HEAD_DIM + 3), 1.0, 0.0)
    q_scale = ATT_HEAD_DIM ** -0.5 * LOG2E

    def normed(x, g):
        sq = x * x
        ss_lo = jnp.sum(jnp.where(lo_half, sq, 0.0), axis=1, keepdims=True)
        ss_hi = jnp.sum(jnp.where(lo_half, 0.0, sq), axis=1, keepdims=True)
        inv = jnp.where(lo_half, lax.rsqrt(ss_lo / ATT_HEAD_DIM + EPS),
                        lax.rsqrt(ss_hi / ATT_HEAD_DIM + EPS))
        return x * inv * g

    for j in range(ATT_HEADS // 2):
        sl = slice(V7X_LANES * j, V7X_LANES * (j + 1))
        qn = normed(q_ref[0, :, sl], qg_ref[...]) * q_scale
        kn = normed(k_ref[0, :, sl], kg_ref[...])
        vp = v_ref[0, :, sl]
        for parity in range(2):
            h = 2 * j + parity
            qh = qn if parity == 0 else pltpu.roll(qn, ATT_HEAD_DIM, axis=1)
            kh = kn if parity == 0 else pltpu.roll(kn, ATT_HEAD_DIM, axis=1)
            bias = jnp.where(lane == ATT_HEAD_DIM, b_hi[:, h:h + 1],
                             jnp.where(lane == ATT_HEAD_DIM + 1, b_mid[:, h:h + 1],
                                       jnp.where(lane == ATT_HEAD_DIM + 2, b_lo[:, h:h + 1], 0.0)))
            qa_ref[0, h] = jnp.where(lo_half, qh, ones3).astype(BF16)
            ka_ref[0, h] = jnp.where(lo_half, kh, bias).astype(BF16)
            if parity == 0:
                va = jnp.where(lo_half, vp, jnp.where(lane == ATT_HEAD_DIM, 1.0, 0.0))
            else:
                va = jnp.where(lo_half, jnp.where(lane == 0, 1.0, 0.0), vp)
            va_ref[0, h] = va.astype(BF16)


def _att_prep(q, k, v, f_raw, qg, kg, fb):
    b, s, _ = q.shape
    ts = TS_PREP
    tok = pl.BlockSpec((1, ts, ATT_WIDTH), lambda i, t: (i, t, 0))
    row = pl.BlockSpec((1, V7X_LANES), lambda i, t: (0, 0))
    head = pl.BlockSpec((1, ATT_HEADS, ts, V7X_LANES), lambda i, t: (i, 0, t, 0))
    head_shape = jax.ShapeDtypeStruct((b, ATT_HEADS, s, V7X_LANES), BF16)
    nbytes = 2 * (3 * ts * ATT_WIDTH * 4 + ts * V7X_LANES * 4 + 3 * ATT_HEADS * ts * V7X_LANES * 2)
    return pl.pallas_call(
        _att_prep_kernel,
        grid=(b, s // ts),
        in_specs=[tok, tok, tok, pl.BlockSpec((1, ts, V7X_LANES), lambda i, t: (i, t, 0)),
                  row, row, row],
        out_specs=[head, head, head],
        out_shape=[head_shape, head_shape, head_shape],
        scratch_shapes=[pltpu.VMEM((1, V7X_LANES), F32)],
        compiler_params=_params(("parallel", "arbitrary"), nbytes),
        name="att_prep",
    )(q, k, v, f_raw, qg, kg, fb)


def _att_kernel(qa_ref, ka_ref, va_ref, z_ref, o_ref, *, tq, tk):
    qi = pl.program_id(2)
    row = lax.broadcasted_iota(jnp.int32, (tq, tq), 0)
    col = lax.broadcasted_iota(jnp.int32, (tq, tq), 1)
    causal = col <= row
    nt = (((1,), (1,)), ((), ()))
    acc_pair = []
    for hh in range(2):
        q = qa_ref[0, hh]
        d0 = pl.multiple_of(qi * tq, tq)
        s = lax.dot_general(q, ka_ref[0, hh, pl.ds(d0, tq), :], nt, preferred_element_type=F32)
        s = jnp.where(causal, s, -jnp.inf)
        m = jnp.max(s, axis=1, keepdims=True)
        p = jnp.exp2(s - m)
        acc = jnp.dot(p.astype(BF16), va_ref[0, hh, pl.ds(d0, tq), :], preferred_element_type=F32)

        def body(j, carry, q=q, hh=hh):
            m, acc = carry
            k0 = pl.multiple_of(j * tk, tk)
            s = lax.dot_general(q, ka_ref[0, hh, pl.ds(k0, tk), :], nt, preferred_element_type=F32)
            m_new = jnp.maximum(m, jnp.max(s, axis=1, keepdims=True))
            p = jnp.exp2(s - m_new)
            pv = jnp.dot(p.astype(BF16), va_ref[0, hh, pl.ds(k0, tk), :], preferred_element_type=F32)
            return m_new, acc * jnp.exp2(m - m_new) + pv

        m, acc = lax.fori_loop(0, qi * (tq // tk), body, (m, acc))
        acc_pair.append(acc)

    lane = lax.broadcasted_iota(jnp.int32, (tq, V7X_LANES), 1)
    even, odd = acc_pair
    o = jnp.where(lane < ATT_HEAD_DIM, even * (1.0 / even[:, ATT_HEAD_DIM:ATT_HEAD_DIM + 1]),
                  odd * (1.0 / odd[:, 0:1]))
    o_ref[0] = (o * _silu(z_ref[0])).astype(o_ref.dtype)


def _attention(qa, ka, va, z_att):
    b, _, s, _ = qa.shape
    tq, tk = TQ_ATT, TK_ATT
    pair_tile = pl.BlockSpec((1, 2, tq, V7X_LANES), lambda i, j, t: (i, j, t, 0))
    pair_full = pl.BlockSpec((1, 2, s, V7X_LANES), lambda i, j, t: (i, j, 0, 0))
    tok = pl.BlockSpec((1, tq, V7X_LANES), lambda i, j, t: (i, t, j))
    nbytes = 2 * (2 * tq * V7X_LANES * 2 + 2 * 2 * s * V7X_LANES * 2 + tq * V7X_LANES * 6) \
        + 4 * tq * max(tq, tk) * 4
    return pl.pallas_call(
        functools.partial(_att_kernel, tq=tq, tk=tk),
        grid=(b, ATT_HEADS // 2, s // tq),
        in_specs=[pair_tile, pair_full, pair_full, tok],
        out_specs=tok,
        out_shape=jax.ShapeDtypeStruct((b, s, ATT_WIDTH), BF16),
        compiler_params=_params(("parallel", "parallel", "arbitrary"), nbytes),
        name="att",
    )(qa, ka, va, z_att)


def _ssd_kernel(xbc_ref, z_ref, dt_ref, cw_ref, cb_ref, dtb_ref, alog_ref, dsk_ref, ng_ref,
                y_ref, xpad_ref, state_ref, *, chunks):
    L = SSD_CHUNK
    n = SSD_STATE
    gw = SSD_WIDTH // SSD_GROUPS

    @pl.when(pl.program_id(1) == 0)
    def _():
        xpad_ref[0:HALO, :] = jnp.zeros((HALO, SSD_CONV_DIM), F32)
        state_ref[...] = jnp.zeros_like(state_ref)

    row = lax.broadcasted_iota(jnp.int32, (L, L), 0)
    col = lax.broadcasted_iota(jnp.int32, (L, L), 1)
    causal = col <= row
    tri = jnp.where(causal, 1.0, 0.0).astype(BF16)
    lo_half = lax.broadcasted_iota(jnp.int32, (L, V7X_LANES), 1) < SSD_HEAD_DIM
    a_neg = -jnp.exp(alog_ref[...])

    for ci in range(chunks):
        rows = slice(ci * L, (ci + 1) * L)
        x_raw = xbc_ref[0, rows, :]
        xpad_ref[HALO:HALO + L, :] = x_raw
        conv = cb_ref[...]
        for kk in range(CONV_WIDTH):
            start = HALO - (CONV_WIDTH - 1) + kk
            conv = conv + cw_ref[kk:kk + 1, :] * xpad_ref[start:start + L, :]
        xpad_ref[0:HALO, :] = x_raw[L - HALO:L, :]
        xbc = _silu(conv)
        xs = xbc[:, :SSD_WIDTH]
        bm = xbc[:, SSD_WIDTH:SSD_WIDTH + SSD_GROUPS * n]
        cm = xbc[:, SSD_WIDTH + SSD_GROUPS * n:]

        dt = _softplus(dt_ref[0, rows, :] + dtb_ref[...])
        acs = _cumsum_rows(dt * a_neg, tri)
        tot = acs[L - 1:L, :]
        e_acs = jnp.exp(acs)
        e_tot = jnp.exp(tot)
        acs_t = acs.T
        dt_t = dt.T
        w_t = (dt * jnp.exp(tot - acs)).T

        y_tiles = []
        for g in range(SSD_GROUPS):
            bg_t = bm[:, n * g:n * (g + 1)].T
            cg = cm[:, n * g:n * (g + 1)].astype(BF16)
            cb = jnp.dot(cg, bg_t.astype(BF16), preferred_element_type=F32)
            s_prev = state_ref[g]
            cs = jnp.dot(cg, s_prev.astype(BF16), preferred_element_type=F32)
            for jj in range(gw // V7X_LANES):
                j = g * (gw // V7X_LANES) + jj
                he, ho = 2 * j, 2 * j + 1
                xp = xs[:, V7X_LANES * j:V7X_LANES * (j + 1)]
                rhs = jnp.concatenate([jnp.where(lo_half, xp, 0.0), jnp.where(lo_half, 0.0, xp)],
                                      axis=0).astype(BF16)

                def scores(h):
                    seg = acs[:, h:h + 1] - acs_t[h:h + 1, :]
                    return cb * jnp.exp(jnp.where(causal, seg, -jnp.inf)) * dt_t[h:h + 1, :]

                lhs = jnp.concatenate([scores(he), scores(ho)], axis=1).astype(BF16)
                y_diag = jnp.dot(lhs, rhs, preferred_element_type=F32)
                e_pair = jnp.where(lo_half, e_acs[:, he:he + 1], e_acs[:, ho:ho + 1])
                y_off = cs[:, V7X_LANES * jj:V7X_LANES * (jj + 1)] * e_pair
                y_tiles.append(y_diag + y_off + xp * dsk_ref[:, V7X_LANES * j:V7X_LANES * (j + 1)])

                lhs_s = jnp.concatenate([bg_t * w_t[he:he + 1, :], bg_t * w_t[ho:ho + 1, :]],
                                        axis=1).astype(BF16)
                s_new = jnp.dot(lhs_s, rhs, preferred_element_type=F32)
                dec = jnp.where(lo_half[0:1], e_tot[:, he:he + 1], e_tot[:, ho:ho + 1])
                state_ref[g, :, V7X_LANES * jj:V7X_LANES * (jj + 1)] = (
                    s_prev[:, V7X_LANES * jj:V7X_LANES * (jj + 1)] * dec + s_new)

        y = jnp.concatenate(y_tiles, axis=1)
        gated = y * _silu(z_ref[0, rows, :])
        outs = []
        for g in range(SSD_GROUPS):
            gg = gated[:, gw * g:gw * (g + 1)]
            ms = jnp.mean(gg * gg, axis=-1, keepdims=True)
            outs.append(gg * lax.rsqrt(ms + EPS))
        y_ref[0, rows, :] = (jnp.concatenate(outs, axis=1) * ng_ref[...]).astype(y_ref.dtype)


def _ssd(xbc, z, dt_raw, conv_w, conv_b, dt_bias, a_log, d_skip, norm_g):
    b, s, _ = xbc.shape
    chunks = SSD_CHUNKS_PER_STEP
    ts = chunks * SSD_CHUNK
    tok = lambda wd: pl.BlockSpec((1, ts, wd), lambda i, t: (i, t, 0))
    par = lambda r, wd: pl.BlockSpec((r, wd), lambda i, t: (0, 0))
    nbytes = 2 * ts * (SSD_CONV_DIM + SSD_WIDTH + V7X_LANES) * 4 + 2 * ts * SSD_WIDTH * 2 \
        + (SSD_CHUNK + HALO) * SSD_CONV_DIM * 4 + SSD_GROUPS * SSD_STATE * SSD_WIDTH * 2 \
        + 8 * SSD_CHUNK * SSD_CONV_DIM * 4
    return pl.pallas_call(
        functools.partial(_ssd_kernel, chunks=chunks),
        grid=(b, s // ts),
        in_specs=[tok(SSD_CONV_DIM), tok(SSD_WIDTH), tok(V7X_LANES),
                  par(CONV_WIDTH, SSD_CONV_DIM), par(1, SSD_CONV_DIM), par(1, V7X_LANES),
                  par(1, V7X_LANES), par(1, SSD_WIDTH), par(1, SSD_WIDTH)],
        out_specs=tok(SSD_WIDTH),
        out_shape=jax.ShapeDtypeStruct((b, s, SSD_WIDTH), BF16),
        scratch_shapes=[pltpu.VMEM((HALO + SSD_CHUNK, SSD_CONV_DIM), F32),
                        pltpu.VMEM((SSD_GROUPS, SSD_STATE, SSD_WIDTH // SSD_GROUPS), F32)],
        compiler_params=_params(("parallel", "arbitrary"), nbytes),
        name="ssd",
    )(xbc, z, dt_raw, conv_w, conv_b, dt_bias, a_log, d_skip, norm_g)


def _lru_kernel(x_ref, z_ref, cw_ref, cb_ref, wg_ref, bg_ref, lam_ref, y_ref,
                xpad_ref, a_ref, b_ref, h_ref):
    ts = x_ref.shape[1]
    first_tile = pl.program_id(1) == 0

    @pl.when(first_tile)
    def _():
        xpad_ref[0:HALO, :] = jnp.zeros((HALO, LRU_WIDTH), F32)
        h_ref[...] = jnp.zeros_like(h_ref)

    x_raw = x_ref[0]
    xpad_ref[HALO:HALO + ts, :] = x_raw
    xc = cb_ref[...]
    for kk in range(CONV_WIDTH):
        start = HALO - (CONV_WIDTH - 1) + kk
        xc = xc + cw_ref[kk:kk + 1, :] * xpad_ref[start:start + ts, :]
    xpad_ref[0:HALO, :] = x_raw[ts - HALO:ts, :]

    gates = jnp.dot(xc.astype(BF16), wg_ref[...], preferred_element_type=F32) + bg_ref[...]
    r = _sigmoid(gates[:, :LRU_WIDTH])
    i = _sigmoid(gates[:, LRU_WIDTH:])
    log_a = -LRU_C * r * _softplus(-lam_ref[...])
    a = jnp.exp(log_a)
    mult = jnp.sqrt(1.0 - jnp.exp(2.0 * log_a))
    seq_start = first_tile & (lax.broadcasted_iota(jnp.int32, (ts, LRU_WIDTH), 0) == 0)
    mult = jnp.where(seq_start, 1.0, mult)
    a_ref[...] = a
    b_ref[...] = mult * (i * xc)

    sub = lax.broadcasted_iota(jnp.int32, (V7X_SUBLANES, LRU_WIDTH), 0)

    def group(gi, h):
        r0 = pl.multiple_of(gi * V7X_SUBLANES, V7X_SUBLANES)
        aa = a_ref[pl.ds(r0, V7X_SUBLANES), :]
        bb = b_ref[pl.ds(r0, V7X_SUBLANES), :]
        for d in (1, 2, 4):
            keep = sub >= d
            bb = jnp.where(keep, aa * pltpu.roll(bb, d, axis=0) + bb, bb)
            aa = jnp.where(keep, aa * pltpu.roll(aa, d, axis=0), aa)
        hh = aa * h + bb
        zz = z_ref[0, pl.ds(r0, V7X_SUBLANES), :]
        y_ref[0, pl.ds(r0, V7X_SUBLANES), :] = (hh * _silu(zz)).astype(y_ref.dtype)
        return hh[V7X_SUBLANES - 1:V7X_SUBLANES, :]

    h_ref[...] = lax.fori_loop(0, ts // V7X_SUBLANES, group, h_ref[...])


def _lru(x_lru, z_lru, conv_w, conv_b, w_gate, b_gate, lam):
    b, s, _ = x_lru.shape
    ts = TS_LRU
    tok = pl.BlockSpec((1, ts, LRU_WIDTH), lambda i, t: (i, t, 0))
    par = lambda r, wd: pl.BlockSpec((r, wd), lambda i, t: (0, 0))
    nbytes = 2 * 2 * ts * LRU_WIDTH * 4 + 2 * ts * LRU_WIDTH * 2 + 2 * LRU_WIDTH * 2 * LRU_WIDTH * 2 \
        + (3 * ts + HALO) * LRU_WIDTH * 4 + 6 * ts * LRU_WIDTH * 4
    return pl.pallas_call(
        _lru_kernel,
        grid=(b, s // ts),
        in_specs=[tok, tok, par(CONV_WIDTH, LRU_WIDTH), par(1, LRU_WIDTH),
                  par(LRU_WIDTH, 2 * LRU_WIDTH), par(1, 2 * LRU_WIDTH), par(1, LRU_WIDTH)],
        out_specs=tok,
        out_shape=jax.ShapeDtypeStruct((b, s, LRU_WIDTH), BF16),
        scratch_shapes=[pltpu.VMEM((HALO + ts, LRU_WIDTH), F32),
                        pltpu.VMEM((ts, LRU_WIDTH), F32),
                        pltpu.VMEM((ts, LRU_WIDTH), F32),
                        pltpu.VMEM((1, LRU_WIDTH), F32)],
        compiler_params=_params(("parallel", "arbitrary"), nbytes),
        name="lru",
    )(x_lru, z_lru, conv_w, conv_b, w_gate, b_gate, lam)


def _outproj_kernel(h_ref, ya_ref, ys_ref, yl_ref, w_ref, o_ref):
    a0, a1, a2 = ATT_WIDTH, ATT_WIDTH + SSD_WIDTH, MIX_WIDTH
    o_ref[...] = (h_ref[...]
                  + jnp.dot(ya_ref[...], w_ref[0:a0, :], preferred_element_type=F32)
                  + jnp.dot(ys_ref[...], w_ref[a0:a1, :], preferred_element_type=F32)
                  + jnp.dot(yl_ref[...], w_ref[a1:a2, :], preferred_element_type=F32))


def _outproj(h, y_att, y_ssd, y_lru, w):
    t = h.shape[0]
    tm = TM_OUTPROJ
    tok = lambda wd: pl.BlockSpec((tm, wd), lambda i: (i, 0))
    nbytes = 2 * (2 * tm * D_MODEL * 4 + tm * MIX_WIDTH * 2 + MIX_WIDTH * D_MODEL * 2)
    return pl.pallas_call(
        _outproj_kernel,
        grid=(t // tm,),
        in_specs=[tok(D_MODEL), tok(ATT_WIDTH), tok(SSD_WIDTH), tok(LRU_WIDTH),
                  pl.BlockSpec((MIX_WIDTH, D_MODEL), lambda i: (0, 0))],
        out_specs=tok(D_MODEL),
        out_shape=jax.ShapeDtypeStruct((t, D_MODEL), F32),
        compiler_params=_params(("parallel",), nbytes),
        name="outproj",
    )(h, y_att, y_ssd, y_lru, w)


def _pad_lanes(v, width=V7X_LANES):
    return jnp.pad(v, (0, width - v.shape[0]))[None, :]


def _regroup_w_in(w):
    parts = dict(zip(("q", "k", "v", "z_att", "f_raw", "z_ssd", "xbc", "dt_raw", "x_lru", "z_lru"),
                     jnp.split(w, [sum(IN_SIZES[:i]) for i in range(1, len(IN_SIZES))], axis=1)))
    cols = []
    for name, width in _SEGS:
        p = parts[name]
        cols.append(jnp.pad(p, ((0, 0), (0, width - p.shape[1]))))
    return jnp.concatenate(cols, axis=1).astype(BF16)


def _block_diag(w):
    nb, d, e = w.shape
    eye = jnp.eye(nb, dtype=w.dtype)
    return jnp.einsum("nde,nm->ndme", w, eye).reshape(nb * d, nb * e)


def kernel(x, norm_g, w_in, q_norm_g, k_norm_g, forget_b, ssd_conv_w, ssd_conv_b, ssd_dt_bias,
           ssd_a_log, ssd_d, ssd_norm_g, lru_conv_w, lru_conv_b, lru_w_a, lru_b_a, lru_w_x, lru_b_x,
           lru_lambda, w_out):
    bsz, seq, d = x.shape
    assert d == D_MODEL and seq % max(TQ_ATT, TS_LRU, TS_PREP, SSD_CHUNKS_PER_STEP * SSD_CHUNK) == 0
    assert (bsz * seq) % max(TM_INPROJ, TM_OUTPROJ) == 0
    depth = w_in.shape[0]
    h = x.astype(F32).reshape(bsz * seq, D_MODEL)
    seq_view = lambda a: a.reshape(bsz, seq, a.shape[-1])
    for l in range(depth):
        q, k, v, z_att, z_ssd, xbc, x_lru, z_lru, f_raw, dt_raw = (
            seq_view(a) for a in _inproj(h, norm_g[l][None, :], _regroup_w_in(w_in[l])))
        qa, ka, va = _att_prep(q, k, v, f_raw, jnp.tile(q_norm_g[l], 2)[None, :],
                               jnp.tile(k_norm_g[l], 2)[None, :], _pad_lanes(forget_b[l]))
        y_att = _attention(qa, ka, va, z_att)
        y_ssd = _ssd(xbc, z_ssd, dt_raw, ssd_conv_w[l], ssd_conv_b[l][None, :],
                     _pad_lanes(ssd_dt_bias[l]), _pad_lanes(ssd_a_log[l]),
                     jnp.repeat(ssd_d[l], SSD_HEAD_DIM)[None, :], ssd_norm_g[l][None, :])
        w_gate = jnp.concatenate([_block_diag(lru_w_a[l]), _block_diag(lru_w_x[l])], axis=1).astype(BF16)
        b_gate = jnp.concatenate([lru_b_a[l], lru_b_x[l]])[None, :]
        y_lru = _lru(x_lru, z_lru, lru_conv_w[l], lru_conv_b[l][None, :], w_gate, b_gate,
                     lru_lambda[l][None, :])
        h = _outproj(h, y_att.reshape(bsz * seq, ATT_WIDTH), y_ssd.reshape(bsz * seq, SSD_WIDTH),
                     y_lru.reshape(bsz * seq, LRU_WIDTH), w_out[l].astype(BF16))
    return h.reshape(bsz, seq, D_MODEL).astype(x.dtype)
```

```python
import functools

import jax
import jax.numpy as jnp
from jax import lax
from jax.experimental import pallas as pl
from jax.experimental.pallas import tpu as pltpu

F32 = jnp.float32
BF16 = jnp.bfloat16

V7X_LANES = 128
V7X_SUBLANES = 8
V7X_VMEM_BYTES = 64 * 1024 * 1024

D_MODEL = 1024
EPS = 1e-6
CONV_WIDTH = 4
MIX_WIDTH = 2 * D_MODEL
ATT_HEAD_DIM = 64
ATT_WIDTH = MIX_WIDTH // 4
ATT_HEADS = ATT_WIDTH // ATT_HEAD_DIM
SSD_HEAD_DIM = 64
SSD_WIDTH = MIX_WIDTH // 2
SSD_HEADS = SSD_WIDTH // SSD_HEAD_DIM
SSD_GROUPS = 2
SSD_STATE = 128
SSD_CHUNK = 128
SSD_CONV_DIM = SSD_WIDTH + 2 * SSD_GROUPS * SSD_STATE
LRU_WIDTH = MIX_WIDTH // 4
LRU_BLOCKS = 8
LRU_C = 8.0
IN_SIZES = (ATT_WIDTH, ATT_WIDTH, ATT_WIDTH, ATT_WIDTH, ATT_HEADS,
            SSD_WIDTH, SSD_CONV_DIM, SSD_HEADS, LRU_WIDTH, LRU_WIDTH)
IN_NAMES = ("q", "k", "v", "z_att", "f_raw", "z_ssd", "xbc", "dt_raw", "x_lru", "z_lru")

LOG2E = 1.4426950408889634
HALO = V7X_SUBLANES

_SEGS = (("q", ATT_WIDTH), ("k", ATT_WIDTH), ("v", ATT_WIDTH), ("z_att", ATT_WIDTH),
         ("z_ssd", SSD_WIDTH), ("xbc", SSD_CONV_DIM), ("x_lru", LRU_WIDTH), ("z_lru", LRU_WIDTH),
         ("f_raw", V7X_LANES), ("dt_raw", V7X_LANES))
_SEG = {}
for _name, _width in _SEGS:
    _SEG[_name] = (sum(w for _, w in _SEGS[:len(_SEG)]), sum(w for _, w in _SEGS[:len(_SEG)]) + _width)
D_IN_PAD = sum(w for _, w in _SEGS)

TQ_ATT = 512
TK_ATT = 256
ATT_HEADS_PER_STEP = 4
ATT_LOOKAHEAD = 6
TM_INPROJ = TK_ATT
INPROJ_CHUNK = 512
SSD_CHUNKS_PER_STEP = 2
TS_LRU = 512
LRU_UNROLL = 4
TM_OUTPROJ = 512


def _vmem_limit(nbytes):
    return int(min(nbytes * 3 // 2 + (8 << 20), V7X_VMEM_BYTES - (6 << 20)))


def _params(semantics, nbytes):
    return pltpu.CompilerParams(dimension_semantics=semantics, vmem_limit_bytes=_vmem_limit(nbytes))


def _sigmoid(x):
    return 1.0 / (1.0 + jnp.exp(-x))


def _silu(x):
    return x * _sigmoid(x)


def _softplus(x):
    return jnp.maximum(x, 0.0) + jnp.log1p(jnp.exp(-jnp.abs(x)))


def _split3(x):
    hi = x.astype(BF16)
    r1 = x - hi.astype(F32)
    mid = r1.astype(BF16)
    lo = (r1 - mid.astype(F32)).astype(BF16)
    return hi, mid, lo


def _tril_ones(n):
    row = lax.broadcasted_iota(jnp.int32, (n, n), 0)
    col = lax.broadcasted_iota(jnp.int32, (n, n), 1)
    return jnp.where(col <= row, 1.0, 0.0).astype(BF16)


def _cumsum_rows(x, tri):
    hi, mid, lo = _split3(x)
    dot = lambda a: jnp.dot(tri, a, preferred_element_type=F32)
    return dot(hi) + dot(mid) + dot(lo)


def _causal_conv(x, pad_ref, w_ref, b_ref, cols):
    rows = x.shape[0]
    pad_ref[HALO:HALO + rows, cols] = x
    xp = pad_ref[:, cols]
    y = b_ref[:, cols] + w_ref[CONV_WIDTH - 1:CONV_WIDTH, cols] * x
    for shift in range(1, CONV_WIDTH):
        kk = CONV_WIDTH - 1 - shift
        y = y + w_ref[kk:kk + 1, cols] * pltpu.roll(xp, shift, axis=0)[HALO:, :]
    pad_ref[0:HALO, cols] = x[rows - HALO:rows, :]
    return y


def _att_operands(q, k, v, f_raw, qg, kg, fb, cum_ref, qa_ref, ka_ref, vt_ref):
    ts = q.shape[0]
    lane = lax.broadcasted_iota(jnp.int32, (ts, V7X_LANES), 1)
    lo_half = lane < ATT_HEAD_DIM

    log_f = -_softplus(-(f_raw + fb))
    c = cum_ref[...] + _cumsum_rows(log_f, _tril_ones(ts))
    cum_ref[...] = c[ts - 1:ts, :]
    b_hi, b_mid, b_lo = (a.astype(F32) for a in _split3(c * (-LOG2E)))

    ones3 = jnp.where((lane >= ATT_HEAD_DIM) & (lane < ATT_HEAD_DIM + 3), 1.0, 0.0)
    q_scale = ATT_HEAD_DIM ** -0.5 * LOG2E

    def normed(x, g):
        sq = x * x
        ss_lo = jnp.sum(jnp.where(lo_half, sq, 0.0), axis=1, keepdims=True)
        ss_hi = jnp.sum(jnp.where(lo_half, 0.0, sq), axis=1, keepdims=True)
        inv = jnp.where(lo_half, lax.rsqrt(ss_lo / ATT_HEAD_DIM + EPS),
                        lax.rsqrt(ss_hi / ATT_HEAD_DIM + EPS))
        return x * inv * g

    for j in range(ATT_HEADS // 2):
        sl = slice(V7X_LANES * j, V7X_LANES * (j + 1))
        qn = normed(q[:, sl], qg) * q_scale
        kn = normed(k[:, sl], kg)
        vp = v[:, sl]
        for parity in range(2):
            h = 2 * j + parity
            qh = qn if parity == 0 else pltpu.roll(qn, ATT_HEAD_DIM, axis=1)
            kh = kn if parity == 0 else pltpu.roll(kn, ATT_HEAD_DIM, axis=1)
            bias = jnp.where(lane == ATT_HEAD_DIM, b_hi[:, h:h + 1],
                             jnp.where(lane == ATT_HEAD_DIM + 1, b_mid[:, h:h + 1],
                                       jnp.where(lane == ATT_HEAD_DIM + 2, b_lo[:, h:h + 1], 0.0)))
            qa_ref[0, h] = jnp.where(lo_half, qh, ones3).astype(BF16)
            ka_ref[0, h] = jnp.where(lo_half, kh, bias).astype(BF16)
            if parity == 0:
                va = jnp.where(lo_half, vp, jnp.where(lane == ATT_HEAD_DIM, 1.0, 0.0))
            else:
                va = jnp.where(lo_half, jnp.where(lane == 0, 1.0, 0.0), vp)
            vt_ref[0, h, 0] = va.T.astype(BF16)


def _inproj_kernel(x_ref, g_ref, w_ref, qg_ref, kg_ref, fb_ref, scw_ref, scb_ref, lcw_ref, lcb_ref,
                   qa_ref, ka_ref, vt_ref, zatt_ref, zssd_ref, xbc_ref, xlru_ref, zlru_ref, dt_ref,
                   cum_ref, spad_ref, lpad_ref):
    @pl.when(pl.program_id(1) == 0)
    def _():
        cum_ref[...] = jnp.zeros_like(cum_ref)
        spad_ref[0:HALO, :] = jnp.zeros((HALO, SSD_CONV_DIM), F32)
        lpad_ref[0:HALO, :] = jnp.zeros((HALO, LRU_WIDTH), F32)

    x = x_ref[0]
    ms = jnp.mean(x * x, axis=-1, keepdims=True)
    u = (x * lax.rsqrt(ms + EPS) * g_ref[...]).astype(BF16)

    def proj(name, cols=None):
        lo, hi = _SEG[name]
        if cols is not None:
            lo, hi = lo + cols.start, lo + cols.stop
        return jnp.dot(u, w_ref[:, lo:hi], preferred_element_type=F32)

    _att_operands(proj("q"), proj("k"), proj("v"), proj("f_raw"), qg_ref[...], kg_ref[...],
                  fb_ref[...], cum_ref, qa_ref, ka_ref, vt_ref)
    plain = [(zssd_ref, "z_ssd", slice(0, INPROJ_CHUNK)), (zssd_ref, "z_ssd", slice(INPROJ_CHUNK, SSD_WIDTH)),
             (zatt_ref, "z_att", slice(0, ATT_WIDTH)), (zlru_ref, "z_lru", slice(0, LRU_WIDTH))]
    for ci in range(SSD_CONV_DIM // INPROJ_CHUNK):
        cols = slice(ci * INPROJ_CHUNK, (ci + 1) * INPROJ_CHUNK)
        xbc_ref[0, :, cols] = _silu(_causal_conv(proj("xbc", cols), spad_ref, scw_ref, scb_ref, cols))
        ref, name, pc = plain[ci]
        ref[0, :, pc] = proj(name, pc)
    xlru_ref[0] = _causal_conv(proj("x_lru"), lpad_ref, lcw_ref, lcb_ref, slice(0, LRU_WIDTH))
    ref, name, pc = plain[-1]
    ref[0, :, pc] = proj(name, pc)
    dt_ref[0] = proj("dt_raw")


def _inproj(h, g, w, qg, kg, fb, ssd_cw, ssd_cb, lru_cw, lru_cb):
    b, s, _ = h.shape
    tm = TM_INPROJ
    tok = lambda wd: pl.BlockSpec((1, tm, wd), lambda i, t: (i, t, 0))
    par = lambda r, wd: pl.BlockSpec((r, wd), lambda i, t: (0, 0))
    head = pl.BlockSpec((1, ATT_HEADS, tm, V7X_LANES), lambda i, t: (i, 0, t, 0))
    head_t = pl.BlockSpec((1, ATT_HEADS, 1, V7X_LANES, tm), lambda i, t: (i, 0, t, 0, 0))
    tok_widths = (ATT_WIDTH, SSD_WIDTH, SSD_CONV_DIM, LRU_WIDTH, LRU_WIDTH, V7X_LANES)
    nbytes = 2 * (tm * D_MODEL * 4 + D_MODEL * D_IN_PAD * 2 + tm * sum(tok_widths) * 4
                  + 3 * ATT_HEADS * tm * V7X_LANES * 2) \
        + (HALO + tm) * (SSD_CONV_DIM + LRU_WIDTH) * 4 + 4 * tm * SSD_CONV_DIM * 4
    return pl.pallas_call(
        _inproj_kernel,
        grid=(b, s // tm),
        in_specs=[tok(D_MODEL), par(1, D_MODEL), par(D_MODEL, D_IN_PAD),
                  par(1, V7X_LANES), par(1, V7X_LANES), par(1, V7X_LANES),
                  par(CONV_WIDTH, SSD_CONV_DIM), par(1, SSD_CONV_DIM),
                  par(CONV_WIDTH, LRU_WIDTH), par(1, LRU_WIDTH)],
        out_specs=[head, head, head_t] + [tok(wd) for wd in tok_widths],
        out_shape=[jax.ShapeDtypeStruct((b, ATT_HEADS, s, V7X_LANES), BF16),
                   jax.ShapeDtypeStruct((b, ATT_HEADS, s, V7X_LANES), BF16),
                   jax.ShapeDtypeStruct((b, ATT_HEADS, s // tm, V7X_LANES, tm), BF16)]
        + [jax.ShapeDtypeStruct((b, s, wd), F32) for wd in tok_widths],
        scratch_shapes=[pltpu.VMEM((1, V7X_LANES), F32),
                        pltpu.VMEM((HALO + tm, SSD_CONV_DIM), F32),
                        pltpu.VMEM((HALO + tm, LRU_WIDTH), F32)],
        compiler_params=_params(("parallel", "arbitrary"), nbytes),
        name="inproj",
    )(h, g, w, qg, kg, fb, ssd_cw, ssd_cb, lru_cw, lru_cb)


def _att_kernel(qa_ref, ka_ref, vt_ref, z_ref, o_ref, *, tq, tk, heads):
    qi = pl.program_id(2)
    subs = tq // tk
    key_i = lax.broadcasted_iota(jnp.int32, (tk, tq), 0)
    qry_i = lax.broadcasted_iota(jnp.int32, (tk, tq), 1)
    nt = (((1,), (1,)), ((), ()))
    chains = [(sub, hh) for sub in range(subs) for hh in range(heads)]

    def step(j, stats, masked):
        stats = list(stats)

        def logits(sub, hh):
            k0 = pl.multiple_of((j * subs + sub) * tk, tk)
            s_t = lax.dot_general(ka_ref[0, hh, pl.ds(k0, tk), :], qa_ref[0, hh], nt,
                                  preferred_element_type=F32)
            if masked:
                s_t = jnp.where(key_i + sub * tk <= qry_i, s_t, -jnp.inf)
            return s_t

        pending = [logits(*c) for c in chains[:ATT_LOOKAHEAD]]
        for idx, (sub, hh) in enumerate(chains):
            s_t = pending.pop(0)
            if idx + ATT_LOOKAHEAD < len(chains):
                pending.append(logits(*chains[idx + ATT_LOOKAHEAD]))
            m, acc = stats[hh]
            m_new = jnp.maximum(m, jnp.max(s_t, axis=0, keepdims=True))
            p_t = jnp.exp2(s_t - m_new).astype(BF16)
            pv = jnp.dot(vt_ref[0, hh, j * subs + sub], p_t, preferred_element_type=F32)
            stats[hh] = (m_new, acc * jnp.exp2(m - m_new) + pv)
        return tuple(stats)

    init = tuple((jnp.full((1, tq), -jnp.inf, F32), jnp.zeros((V7X_LANES, tq), F32))
                 for _ in range(heads))
    stats = step(qi, init, True)
    stats = lax.fori_loop(0, qi, lambda j, c: step(j, c, False), stats)

    sub_i = lax.broadcasted_iota(jnp.int32, (V7X_LANES, tq), 0)
    for pr in range(heads // 2):
        even, odd = stats[2 * pr][1], stats[2 * pr + 1][1]
        o_t = jnp.where(sub_i < ATT_HEAD_DIM, even * (1.0 / even[ATT_HEAD_DIM:ATT_HEAD_DIM + 1, :]),
                        odd * (1.0 / odd[0:1, :]))
        lanes = slice(V7X_LANES * pr, V7X_LANES * (pr + 1))
        o_ref[0, :, lanes] = (o_t.T * _silu(z_ref[0, :, lanes])).astype(o_ref.dtype)


def _attention(qa, ka, vt, z_att):
    b, _, s, _ = qa.shape
    tq, tk, hp = TQ_ATT, TK_ATT, ATT_HEADS_PER_STEP
    assert tq % tk == 0 and hp % 2 == 0 and ATT_HEADS % hp == 0
    wd = hp * ATT_HEAD_DIM
    q_tile = pl.BlockSpec((1, hp, tq, V7X_LANES), lambda i, j, t: (i, j, t, 0))
    k_full = pl.BlockSpec((1, hp, s, V7X_LANES), lambda i, j, t: (i, j, 0, 0))
    v_full = pl.BlockSpec((1, hp, s // tk, V7X_LANES, tk), lambda i, j, t: (i, j, 0, 0, 0))
    tok = pl.BlockSpec((1, tq, wd), lambda i, j, t: (i, t, j))
    nbytes = 2 * (hp * tq * V7X_LANES * 2 + 2 * hp * s * V7X_LANES * 2 + tq * wd * 6) \
        + (2 * ATT_LOOKAHEAD + 4) * tq * tk * 4
    return pl.pallas_call(
        functools.partial(_att_kernel, tq=tq, tk=tk, heads=hp),
        grid=(b, ATT_HEADS // hp, s // tq),
        in_specs=[q_tile, k_full, v_full, tok],
        out_specs=tok,
        out_shape=jax.ShapeDtypeStruct((b, s, ATT_WIDTH), BF16),
        compiler_params=_params(("parallel", "parallel", "arbitrary"), nbytes),
        name="att",
    )(qa, ka, vt, z_att)


def _ssd_kernel(xbc_ref, z_ref, dt_ref, dtb_ref, alog_ref, dsk_ref, ng_ref, y_ref, state_ref,
                *, chunks):
    L = SSD_CHUNK
    n = SSD_STATE
    gw = SSD_WIDTH // SSD_GROUPS

    @pl.when(pl.program_id(1) == 0)
    def _():
        state_ref[...] = jnp.zeros_like(state_ref)

    row = lax.broadcasted_iota(jnp.int32, (L, L), 0)
    col = lax.broadcasted_iota(jnp.int32, (L, L), 1)
    causal = col <= row
    tri = jnp.where(causal, 1.0, 0.0).astype(BF16)
    lo_half = lax.broadcasted_iota(jnp.int32, (L, V7X_LANES), 1) < SSD_HEAD_DIM
    a_neg = -jnp.exp(alog_ref[...])

    for ci in range(chunks):
        rows = slice(ci * L, (ci + 1) * L)
        dt = _softplus(dt_ref[0, rows, :] + dtb_ref[...])
        acs = _cumsum_rows(dt * a_neg, tri)
        tot = acs[L - 1:L, :]
        e_acs = jnp.exp(acs)
        e_tot = jnp.exp(tot)
        acs_t = acs.T
        dt_t = dt.T
        w_t = (dt * jnp.exp(tot - acs)).T

        y_tiles = []
        for g in range(SSD_GROUPS):
            b_lo = SSD_WIDTH + n * g
            c_lo = SSD_WIDTH + SSD_GROUPS * n + n * g
            bg_t = xbc_ref[0, rows, b_lo:b_lo + n].T
            cg = xbc_ref[0, rows, c_lo:c_lo + n].astype(BF16)
            cb = jnp.dot(cg, bg_t.astype(BF16), preferred_element_type=F32)
            s_prev = state_ref[g]
            cs = jnp.dot(cg, s_prev.astype(BF16), preferred_element_type=F32)
            for jj in range(gw // V7X_LANES):
                j = g * (gw // V7X_LANES) + jj
                he, ho = 2 * j, 2 * j + 1
                xp = xbc_ref[0, rows, V7X_LANES * j:V7X_LANES * (j + 1)]
                rhs = jnp.concatenate([jnp.where(lo_half, xp, 0.0), jnp.where(lo_half, 0.0, xp)],
                                      axis=0).astype(BF16)

                def scores(h):
                    seg = acs[:, h:h + 1] - acs_t[h:h + 1, :]
                    return cb * jnp.exp(jnp.where(causal, seg, -jnp.inf)) * dt_t[h:h + 1, :]

                lhs = jnp.concatenate([scores(he), scores(ho)], axis=1).astype(BF16)
                y_diag = jnp.dot(lhs, rhs, preferred_element_type=F32)
                e_pair = jnp.where(lo_half, e_acs[:, he:he + 1], e_acs[:, ho:ho + 1])
                y_off = cs[:, V7X_LANES * jj:V7X_LANES * (jj + 1)] * e_pair
                y_tiles.append(y_diag + y_off + xp * dsk_ref[:, V7X_LANES * j:V7X_LANES * (j + 1)])

                lhs_s = jnp.concatenate([bg_t * w_t[he:he + 1, :], bg_t * w_t[ho:ho + 1, :]],
                                        axis=1).astype(BF16)
                s_new = jnp.dot(lhs_s, rhs, preferred_element_type=F32)
                dec = jnp.where(lo_half[0:1], e_tot[:, he:he + 1], e_tot[:, ho:ho + 1])
                state_ref[g, :, V7X_LANES * jj:V7X_LANES * (jj + 1)] = (
                    s_prev[:, V7X_LANES * jj:V7X_LANES * (jj + 1)] * dec + s_new)

        y = jnp.concatenate(y_tiles, axis=1)
        gated = y * _silu(z_ref[0, rows, :])
        outs = []
        for g in range(SSD_GROUPS):
            gg = gated[:, gw * g:gw * (g + 1)]
            ms = jnp.mean(gg * gg, axis=-1, keepdims=True)
            outs.append(gg * lax.rsqrt(ms + EPS))
        y_ref[0, rows, :] = (jnp.concatenate(outs, axis=1) * ng_ref[...]).astype(y_ref.dtype)


def _ssd(xbc, z, dt_raw, dt_bias, a_log, d_skip, norm_g):
    b, s, _ = xbc.shape
    chunks = SSD_CHUNKS_PER_STEP
    ts = chunks * SSD_CHUNK
    tok = lambda wd: pl.BlockSpec((1, ts, wd), lambda i, t: (i, t, 0))
    par = lambda r, wd: pl.BlockSpec((r, wd), lambda i, t: (0, 0))
    nbytes = 2 * ts * (SSD_CONV_DIM + SSD_WIDTH + V7X_LANES) * 4 + 2 * ts * SSD_WIDTH * 2 \
        + SSD_GROUPS * SSD_STATE * SSD_WIDTH * 2 + 8 * SSD_CHUNK * SSD_CONV_DIM * 4
    return pl.pallas_call(
        functools.partial(_ssd_kernel, chunks=chunks),
        grid=(b, s // ts),
        in_specs=[tok(SSD_CONV_DIM), tok(SSD_WIDTH), tok(V7X_LANES),
                  par(1, V7X_LANES), par(1, V7X_LANES), par(1, SSD_WIDTH), par(1, SSD_WIDTH)],
        out_specs=tok(SSD_WIDTH),
        out_shape=jax.ShapeDtypeStruct((b, s, SSD_WIDTH), BF16),
        scratch_shapes=[pltpu.VMEM((SSD_GROUPS, SSD_STATE, SSD_WIDTH // SSD_GROUPS), F32)],
        compiler_params=_params(("parallel", "arbitrary"), nbytes),
        name="ssd",
    )(xbc, z, dt_raw, dt_bias, a_log, d_skip, norm_g)


def _lru_kernel(x_ref, z_ref, wg_ref, bg_ref, lam_ref, y_ref, a_ref, b_ref, h_ref):
    ts = x_ref.shape[1]
    groups = ts // V7X_SUBLANES
    first_tile = pl.program_id(1) == 0

    @pl.when(first_tile)
    def _():
        h_ref[...] = jnp.zeros_like(h_ref)

    xc = x_ref[0]
    gates = jnp.dot(xc.astype(BF16), wg_ref[...], preferred_element_type=F32) + bg_ref[...]
    r = _sigmoid(gates[:, :LRU_WIDTH])
    i = _sigmoid(gates[:, LRU_WIDTH:])
    log_a = -LRU_C * r * _softplus(-lam_ref[...])
    a = jnp.exp(log_a)
    mult = jnp.sqrt(1.0 - jnp.exp(2.0 * log_a))
    seq_start = first_tile & (lax.broadcasted_iota(jnp.int32, (ts, LRU_WIDTH), 0) == 0)
    mult = jnp.where(seq_start, 1.0, mult)
    b = mult * (i * xc)

    aa = a.reshape(groups, V7X_SUBLANES, LRU_WIDTH)
    bb = b.reshape(groups, V7X_SUBLANES, LRU_WIDTH)
    sub = lax.broadcasted_iota(jnp.int32, (groups, V7X_SUBLANES, LRU_WIDTH), 1)
    for d in (1, 2, 4):
        keep = sub >= d
        bb = jnp.where(keep, aa * pltpu.roll(bb, d, axis=1) + bb, bb)
        aa = jnp.where(keep, aa * pltpu.roll(aa, d, axis=1), aa)
    a_ref[...] = aa.reshape(ts, LRU_WIDTH)
    b_ref[...] = bb.reshape(ts, LRU_WIDTH)

    def group(gi, h):
        r0 = pl.multiple_of(gi * V7X_SUBLANES, V7X_SUBLANES)
        rows = pl.ds(r0, V7X_SUBLANES)
        hh = a_ref[rows, :] * h + b_ref[rows, :]
        b_ref[rows, :] = hh
        return hh[V7X_SUBLANES - 1:V7X_SUBLANES, :]

    h_ref[...] = lax.fori_loop(0, groups, group, h_ref[...], unroll=LRU_UNROLL)
    y_ref[0] = (b_ref[...] * _silu(z_ref[0])).astype(y_ref.dtype)


def _lru(x_lru, z_lru, w_gate, b_gate, lam):
    b, s, _ = x_lru.shape
    ts = TS_LRU
    tok = pl.BlockSpec((1, ts, LRU_WIDTH), lambda i, t: (i, t, 0))
    par = lambda r, wd: pl.BlockSpec((r, wd), lambda i, t: (0, 0))
    nbytes = 2 * 2 * ts * LRU_WIDTH * 4 + 2 * ts * LRU_WIDTH * 2 + 2 * LRU_WIDTH * 2 * LRU_WIDTH * 2 \
        + 2 * ts * LRU_WIDTH * 4 + 8 * ts * LRU_WIDTH * 4
    return pl.pallas_call(
        _lru_kernel,
        grid=(b, s // ts),
        in_specs=[tok, tok, par(LRU_WIDTH, 2 * LRU_WIDTH), par(1, 2 * LRU_WIDTH), par(1, LRU_WIDTH)],
        out_specs=tok,
        out_shape=jax.ShapeDtypeStruct((b, s, LRU_WIDTH), BF16),
        scratch_shapes=[pltpu.VMEM((ts, LRU_WIDTH), F32),
                        pltpu.VMEM((ts, LRU_WIDTH), F32),
                        pltpu.VMEM((1, LRU_WIDTH), F32)],
        compiler_params=_params(("parallel", "arbitrary"), nbytes),
        name="lru",
    )(x_lru, z_lru, w_gate, b_gate, lam)


def _outproj_kernel(h_ref, ya_ref, ys_ref, yl_ref, w_ref, o_ref):
    a0, a1, a2 = ATT_WIDTH, ATT_WIDTH + SSD_WIDTH, MIX_WIDTH
    o_ref[...] = (h_ref[...]
                  + jnp.dot(ya_ref[...], w_ref[0:a0, :], preferred_element_type=F32)
                  + jnp.dot(ys_ref[...], w_ref[a0:a1, :], preferred_element_type=F32)
                  + jnp.dot(yl_ref[...], w_ref[a1:a2, :], preferred_element_type=F32))


def _outproj(h, y_att, y_ssd, y_lru, w):
    t = h.shape[0]
    tm = TM_OUTPROJ
    tok = lambda wd: pl.BlockSpec((tm, wd), lambda i: (i, 0))
    nbytes = 2 * (2 * tm * D_MODEL * 4 + tm * MIX_WIDTH * 2 + MIX_WIDTH * D_MODEL * 2)
    return pl.pallas_call(
        _outproj_kernel,
        grid=(t // tm,),
        in_specs=[tok(D_MODEL), tok(ATT_WIDTH), tok(SSD_WIDTH), tok(LRU_WIDTH),
                  pl.BlockSpec((MIX_WIDTH, D_MODEL), lambda i: (0, 0))],
        out_specs=tok(D_MODEL),
        out_shape=jax.ShapeDtypeStruct((t, D_MODEL), F32),
        compiler_params=_params(("parallel",), nbytes),
        name="outproj",
    )(h, y_att, y_ssd, y_lru, w)


def _pad_lanes(v, width=V7X_LANES):
    return jnp.pad(v, (0, width - v.shape[0]))[None, :]


def _regroup_w_in(w):
    parts = dict(zip(IN_NAMES, jnp.split(w, [sum(IN_SIZES[:i]) for i in range(1, len(IN_SIZES))],
                                         axis=1)))
    cols = [jnp.pad(parts[name], ((0, 0), (0, width - parts[name].shape[1]))) for name, width in _SEGS]
    return jnp.concatenate(cols, axis=1).astype(BF16)


def _block_diag(w):
    nb, d, e = w.shape
    eye = jnp.eye(nb, dtype=w.dtype)
    return jnp.einsum("nde,nm->ndme", w, eye).reshape(nb * d, nb * e)


def kernel(x, norm_g, w_in, q_norm_g, k_norm_g, forget_b, ssd_conv_w, ssd_conv_b, ssd_dt_bias,
           ssd_a_log, ssd_d, ssd_norm_g, lru_conv_w, lru_conv_b, lru_w_a, lru_b_a, lru_w_x, lru_b_x,
           lru_lambda, w_out):
    bsz, seq, d = x.shape
    assert d == D_MODEL and seq % max(TQ_ATT, TM_INPROJ, TS_LRU, SSD_CHUNKS_PER_STEP * SSD_CHUNK) == 0
    assert (bsz * seq) % TM_OUTPROJ == 0
    depth = w_in.shape[0]
    h = x.astype(F32).reshape(bsz * seq, D_MODEL)
    for l in range(depth):
        qa, ka, vt, z_att, z_ssd, xbc, x_lru, z_lru, dt_raw = _inproj(
            h.reshape(bsz, seq, D_MODEL), norm_g[l][None, :], _regroup_w_in(w_in[l]),
            jnp.tile(q_norm_g[l], 2)[None, :], jnp.tile(k_norm_g[l], 2)[None, :],
            _pad_lanes(forget_b[l]), ssd_conv_w[l], ssd_conv_b[l][None, :],
            lru_conv_w[l], lru_conv_b[l][None, :])
        y_att = _attention(qa, ka, vt, z_att)
        y_ssd = _ssd(xbc, z_ssd, dt_raw, _pad_lanes(ssd_dt_bias[l]), _pad_lanes(ssd_a_log[l]),
                     jnp.repeat(ssd_d[l], SSD_HEAD_DIM)[None, :], ssd_norm_g[l][None, :])
        w_gate = jnp.concatenate([_block_diag(lru_w_a[l]), _block_diag(lru_w_x[l])], axis=1).astype(BF16)
        b_gate = jnp.concatenate([lru_b_a[l], lru_b_x[l]])[None, :]
        y_lru = _lru(x_lru, z_lru, w_gate, b_gate, lru_lambda[l][None, :])
        h = _outproj(h, y_att.reshape(bsz * seq, ATT_WIDTH), y_ssd.reshape(bsz * seq, SSD_WIDTH),
                     y_lru.reshape(bsz * seq, LRU_WIDTH), w_out[l].astype(BF16))
    return h.reshape(bsz, seq, D_MODEL).astype(x.dtype)
```

```python
import functools

import jax
import jax.numpy as jnp
from jax import lax
from jax.experimental import pallas as pl
from jax.experimental.pallas import tpu as pltpu

F32 = jnp.float32
BF16 = jnp.bfloat16

V7X_LANES = 128
V7X_SUBLANES = 8
V7X_VMEM_BYTES = 64 * 1024 * 1024

D_MODEL = 1024
EPS = 1e-6
CONV_WIDTH = 4
MIX_WIDTH = 2 * D_MODEL
ATT_HEAD_DIM = 64
ATT_WIDTH = MIX_WIDTH // 4
ATT_HEADS = ATT_WIDTH // ATT_HEAD_DIM
SSD_HEAD_DIM = 64
SSD_WIDTH = MIX_WIDTH // 2
SSD_HEADS = SSD_WIDTH // SSD_HEAD_DIM
SSD_GROUPS = 2
SSD_STATE = 128
SSD_CHUNK = 128
SSD_CONV_DIM = SSD_WIDTH + 2 * SSD_GROUPS * SSD_STATE
LRU_WIDTH = MIX_WIDTH // 4
LRU_BLOCKS = 8
LRU_C = 8.0
IN_SIZES = (ATT_WIDTH, ATT_WIDTH, ATT_WIDTH, ATT_WIDTH, ATT_HEADS,
            SSD_WIDTH, SSD_CONV_DIM, SSD_HEADS, LRU_WIDTH, LRU_WIDTH)
IN_NAMES = ("q", "k", "v", "z_att", "f_raw", "z_ssd", "xbc", "dt_raw", "x_lru", "z_lru")

LOG2E = 1.4426950408889634
HALO = V7X_SUBLANES

_SEGS = (("q", ATT_WIDTH), ("k", ATT_WIDTH), ("v", ATT_WIDTH), ("z_att", ATT_WIDTH),
         ("z_ssd", SSD_WIDTH), ("xbc", SSD_CONV_DIM), ("x_lru", LRU_WIDTH), ("z_lru", LRU_WIDTH),
         ("f_raw", V7X_LANES), ("dt_raw", V7X_LANES))
_SEG = {}
for _name, _width in _SEGS:
    _SEG[_name] = (sum(w for _, w in _SEGS[:len(_SEG)]), sum(w for _, w in _SEGS[:len(_SEG)]) + _width)
D_IN_PAD = sum(w for _, w in _SEGS)

TQ_ATT = 512
TK_ATT = 256
ATT_VT_ROWS = 80
ATT_HEADS_PER_STEP = 8
ATT_LOOKAHEAD = 3
TM_INPROJ = 256
INPROJ_CHUNK = 512
SSD_CHUNKS_PER_STEP = 2
TS_LRU = 512
LRU_UNROLL = 4
TM_OUTPROJ = 512


def _vmem_limit(nbytes):
    return int(min(nbytes * 3 // 2 + (8 << 20), V7X_VMEM_BYTES - (6 << 20)))


def _params(semantics, nbytes):
    return pltpu.CompilerParams(dimension_semantics=semantics, vmem_limit_bytes=_vmem_limit(nbytes))


def _sigmoid(x):
    return 1.0 / (1.0 + jnp.exp(-x))


def _silu(x):
    return x * _sigmoid(x)


def _softplus(x):
    return jnp.maximum(x, 0.0) + jnp.log1p(jnp.exp(-jnp.abs(x)))


def _split3(x):
    hi = x.astype(BF16)
    r1 = x - hi.astype(F32)
    mid = r1.astype(BF16)
    lo = (r1 - mid.astype(F32)).astype(BF16)
    return hi, mid, lo


def _tril_ones(n):
    row = lax.broadcasted_iota(jnp.int32, (n, n), 0)
    col = lax.broadcasted_iota(jnp.int32, (n, n), 1)
    return jnp.where(col <= row, 1.0, 0.0).astype(BF16)


def _cumsum_rows(x, tri):
    hi, mid, lo = _split3(x)
    dot = lambda a: jnp.dot(tri, a, preferred_element_type=F32)
    return dot(hi) + dot(mid) + dot(lo)


def _causal_conv(x, pad_ref, w_ref, b_ref, cols):
    rows = x.shape[0]
    pad_ref[HALO:HALO + rows, cols] = x
    xp = pad_ref[:, cols]
    y = b_ref[:, cols] + w_ref[CONV_WIDTH - 1:CONV_WIDTH, cols] * x
    for shift in range(1, CONV_WIDTH):
        kk = CONV_WIDTH - 1 - shift
        y = y + w_ref[kk:kk + 1, cols] * pltpu.roll(xp, shift, axis=0)[HALO:, :]
    pad_ref[0:HALO, cols] = x[rows - HALO:rows, :]
    return y


def _att_operands(qkvf_ref, qg, kg, fb, cum_ref, qa_ref, ka_ref, vt_ref):
    ts = qkvf_ref.shape[0]
    lane = lax.broadcasted_iota(jnp.int32, (ts, V7X_LANES), 1)
    lo_half = lane < ATT_HEAD_DIM
    f_raw = qkvf_ref[:, 3 * ATT_WIDTH:]

    log_f = -_softplus(-(f_raw + fb))
    c = cum_ref[...] + _cumsum_rows(log_f, _tril_ones(ts))
    cum_ref[...] = c[ts - 1:ts, :]
    b_hi, b_mid, b_lo = (a.astype(F32) for a in _split3(c * (-LOG2E)))

    ones3 = jnp.where((lane >= ATT_HEAD_DIM) & (lane < ATT_HEAD_DIM + 3), 1.0, 0.0)
    q_scale = ATT_HEAD_DIM ** -0.5 * LOG2E

    def normed(x, g):
        sq = x * x
        ss_lo = jnp.sum(jnp.where(lo_half, sq, 0.0), axis=1, keepdims=True)
        ss_hi = jnp.sum(jnp.where(lo_half, 0.0, sq), axis=1, keepdims=True)
        inv = jnp.where(lo_half, lax.rsqrt(ss_lo / ATT_HEAD_DIM + EPS),
                        lax.rsqrt(ss_hi / ATT_HEAD_DIM + EPS))
        return x * inv * g

    pad_rows = ATT_VT_ROWS - ATT_HEAD_DIM
    ones_row = jnp.where(lax.broadcasted_iota(jnp.int32, (pad_rows, ts), 0) == 0, 1.0, 0.0)
    for j in range(ATT_HEADS // 2):
        sl = slice(V7X_LANES * j, V7X_LANES * (j + 1))
        qn = normed(qkvf_ref[:, sl], qg) * q_scale
        kn = normed(qkvf_ref[:, ATT_WIDTH + sl.start:ATT_WIDTH + sl.stop], kg)
        vp_t = qkvf_ref[:, 2 * ATT_WIDTH + sl.start:2 * ATT_WIDTH + sl.stop].T
        vt_ref[0, 2 * j, 0] = jnp.concatenate([vp_t[:ATT_HEAD_DIM, :], ones_row], axis=0).astype(BF16)
        vt_ref[0, 2 * j + 1, 0] = jnp.concatenate([ones_row, vp_t[ATT_HEAD_DIM:, :]],
                                                  axis=0).astype(BF16)
        for parity in range(2):
            h = 2 * j + parity
            qh = qn if parity == 0 else pltpu.roll(qn, ATT_HEAD_DIM, axis=1)
            kh = kn if parity == 0 else pltpu.roll(kn, ATT_HEAD_DIM, axis=1)
            bias = jnp.where(lane == ATT_HEAD_DIM, b_hi[:, h:h + 1],
                             jnp.where(lane == ATT_HEAD_DIM + 1, b_mid[:, h:h + 1],
                                       jnp.where(lane == ATT_HEAD_DIM + 2, b_lo[:, h:h + 1], 0.0)))
            qa_ref[0, h] = jnp.where(lo_half, qh, ones3).astype(BF16)
            ka_ref[0, h] = jnp.where(lo_half, kh, bias).astype(BF16)


def _inproj_kernel(x_ref, g_ref, w_ref, qg_ref, kg_ref, fb_ref, scw_ref, scb_ref, lcw_ref, lcb_ref,
                   qa_ref, ka_ref, vt_ref, zatt_ref, zssd_ref, xbc_ref, xlru_ref, zlru_ref, dt_ref,
                   cum_ref, spad_ref, lpad_ref, qkvf_ref):
    @pl.when(pl.program_id(1) == 0)
    def _():
        cum_ref[...] = jnp.zeros_like(cum_ref)
        spad_ref[0:HALO, :] = jnp.zeros((HALO, SSD_CONV_DIM), F32)
        lpad_ref[0:HALO, :] = jnp.zeros((HALO, LRU_WIDTH), F32)

    x = x_ref[0]
    ms = jnp.mean(x * x, axis=-1, keepdims=True)
    u = (x * lax.rsqrt(ms + EPS) * g_ref[...]).astype(BF16)

    def proj(name, cols=None):
        lo, hi = _SEG[name]
        if cols is not None:
            lo, hi = lo + cols.start, lo + cols.stop
        return jnp.dot(u, w_ref[:, lo:hi], preferred_element_type=F32)

    for si, name in enumerate(("q", "k", "v")):
        qkvf_ref[:, si * ATT_WIDTH:(si + 1) * ATT_WIDTH] = proj(name)
    qkvf_ref[:, 3 * ATT_WIDTH:] = proj("f_raw")
    _att_operands(qkvf_ref, qg_ref[...], kg_ref[...], fb_ref[...], cum_ref, qa_ref, ka_ref, vt_ref)
    plain = [(zssd_ref, "z_ssd", slice(0, INPROJ_CHUNK)), (zssd_ref, "z_ssd", slice(INPROJ_CHUNK, SSD_WIDTH)),
             (zatt_ref, "z_att", slice(0, ATT_WIDTH)), (zlru_ref, "z_lru", slice(0, LRU_WIDTH))]
    for ci in range(SSD_CONV_DIM // INPROJ_CHUNK):
        cols = slice(ci * INPROJ_CHUNK, (ci + 1) * INPROJ_CHUNK)
        xbc_ref[0, :, cols] = _silu(_causal_conv(proj("xbc", cols), spad_ref, scw_ref, scb_ref, cols))
        ref, name, pc = plain[ci]
        ref[0, :, pc] = proj(name, pc)
    xlru_ref[0] = _causal_conv(proj("x_lru"), lpad_ref, lcw_ref, lcb_ref, slice(0, LRU_WIDTH))
    ref, name, pc = plain[-1]
    ref[0, :, pc] = proj(name, pc)
    dt_ref[0] = proj("dt_raw")


def _inproj(h, g, w, qg, kg, fb, ssd_cw, ssd_cb, lru_cw, lru_cb):
    b, s, _ = h.shape
    tm = TM_INPROJ
    tok = lambda wd: pl.BlockSpec((1, tm, wd), lambda i, t: (i, t, 0))
    par = lambda r, wd: pl.BlockSpec((r, wd), lambda i, t: (0, 0))
    head = pl.BlockSpec((1, ATT_HEADS, tm, V7X_LANES), lambda i, t: (i, 0, t, 0))
    head_t = pl.BlockSpec((1, ATT_HEADS, 1, ATT_VT_ROWS, tm), lambda i, t: (i, 0, t, 0, 0))
    tok_widths = (ATT_WIDTH, SSD_WIDTH, SSD_CONV_DIM, LRU_WIDTH, LRU_WIDTH, V7X_LANES)
    nbytes = 2 * (tm * D_MODEL * 4 + D_MODEL * D_IN_PAD * 2 + tm * sum(tok_widths) * 4
                  + 3 * ATT_HEADS * tm * V7X_LANES * 2) \
        + (HALO + tm) * (SSD_CONV_DIM + LRU_WIDTH) * 4 + 4 * tm * SSD_CONV_DIM * 4
    return pl.pallas_call(
        _inproj_kernel,
        grid=(b, s // tm),
        in_specs=[tok(D_MODEL), par(1, D_MODEL), par(D_MODEL, D_IN_PAD),
                  par(1, V7X_LANES), par(1, V7X_LANES), par(1, V7X_LANES),
                  par(CONV_WIDTH, SSD_CONV_DIM), par(1, SSD_CONV_DIM),
                  par(CONV_WIDTH, LRU_WIDTH), par(1, LRU_WIDTH)],
        out_specs=[head, head, head_t] + [tok(wd) for wd in tok_widths],
        out_shape=[jax.ShapeDtypeStruct((b, ATT_HEADS, s, V7X_LANES), BF16),
                   jax.ShapeDtypeStruct((b, ATT_HEADS, s, V7X_LANES), BF16),
                   jax.ShapeDtypeStruct((b, ATT_HEADS, s // tm, ATT_VT_ROWS, tm), BF16)]
        + [jax.ShapeDtypeStruct((b, s, wd), F32) for wd in tok_widths],
        scratch_shapes=[pltpu.VMEM((1, V7X_LANES), F32),
                        pltpu.VMEM((HALO + tm, SSD_CONV_DIM), F32),
                        pltpu.VMEM((HALO + tm, LRU_WIDTH), F32),
                        pltpu.VMEM((tm, 3 * ATT_WIDTH + V7X_LANES), F32)],
        compiler_params=_params(("parallel", "arbitrary"), nbytes),
        name="inproj",
    )(h, g, w, qg, kg, fb, ssd_cw, ssd_cb, lru_cw, lru_cb)


def _att_kernel(qa_ref, ka_ref, vt_ref, z_ref, o_ref, s_ref, m_ref, acc_ref, *, tq, tk, tv, heads):
    qi = pl.program_id(2)
    subs = tq // tk
    nt = (((1,), (1,)), ((), ()))
    chains = [(sub, hh) for sub in range(subs) for hh in range(heads)]
    slots = ATT_LOOKAHEAD + 1

    m_ref[...] = jnp.full(m_ref.shape, -jnp.inf, F32)
    acc_ref[...] = jnp.zeros(acc_ref.shape, F32)

    assert len(chains) % slots == 0 and ATT_LOOKAHEAD <= heads

    def issue_logits(j, idx, q0=0):
        sub, hh = chains[idx]
        k0 = pl.multiple_of((j * subs + sub) * tk, tk)
        s_ref[idx % slots, :, q0:] = lax.dot_general(
            ka_ref[0, hh, pl.ds(k0, tk), :], qa_ref[0, hh, q0:, :], nt,
            preferred_element_type=F32)

    def step(j, diagonal):
        for idx, (sub, hh) in enumerate(chains):
            q0 = sub * tk if diagonal else 0
            s_t = s_ref[idx % slots, :, q0:]
            ahead = idx + ATT_LOOKAHEAD
            if ahead < len(chains):
                issue_logits(j, ahead, chains[ahead][0] * tk if diagonal else 0)
            elif not diagonal:
                issue_logits(j + 1, ahead - len(chains))
            if diagonal:
                key_r = lax.broadcasted_iota(jnp.int32, s_t.shape, 0)
                qry_c = lax.broadcasted_iota(jnp.int32, s_t.shape, 1)
                s_t = jnp.where(key_r + (sub * tk - q0) <= qry_c, s_t, -jnp.inf)
            m = m_ref[hh, :, q0:]
            m_new = jnp.maximum(m, jnp.max(s_t, axis=0, keepdims=True))
            m_ref[hh, :, q0:] = m_new
            p_t = jnp.exp2(s_t - m_new).astype(BF16)
            acc = acc_ref[hh, :, q0:] * jnp.exp2(m - m_new)
            for vb in range(tk // tv):
                acc = acc + jnp.dot(vt_ref[0, hh, (j * subs + sub) * (tk // tv) + vb],
                                    p_t[vb * tv:(vb + 1) * tv, :], preferred_element_type=F32)
            acc_ref[hh, :, q0:] = acc

    for idx in range(ATT_LOOKAHEAD):
        issue_logits(0, idx)

    @pl.loop(0, qi)
    def _(j):
        step(j, False)

    step(qi, True)

    lead = ATT_VT_ROWS - ATT_HEAD_DIM
    for pr in range(heads // 2):
        even, odd = acc_ref[2 * pr], acc_ref[2 * pr + 1]
        o_t = jnp.concatenate(
            [even[0:ATT_HEAD_DIM, :] * (1.0 / even[ATT_HEAD_DIM:ATT_HEAD_DIM + 1, :]),
             odd[lead:, :] * (1.0 / odd[0:1, :])], axis=0)
        lanes = slice(V7X_LANES * pr, V7X_LANES * (pr + 1))
        o_ref[0, :, lanes] = (o_t.T * _silu(z_ref[0, :, lanes])).astype(o_ref.dtype)


def _attention(qa, ka, vt, z_att):
    b, _, s, _ = qa.shape
    tq, tk, hp = TQ_ATT, TK_ATT, ATT_HEADS_PER_STEP
    assert tq % tk == 0 and hp % 2 == 0 and ATT_HEADS % hp == 0
    wd = hp * ATT_HEAD_DIM
    q_tile = pl.BlockSpec((1, hp, tq, V7X_LANES), lambda i, j, t: (i, j, t, 0))
    k_full = pl.BlockSpec((1, hp, s, V7X_LANES), lambda i, j, t: (i, j, 0, 0))
    tv = vt.shape[-1]
    assert tk % tv == 0
    v_full = pl.BlockSpec((1, hp, s // tv, ATT_VT_ROWS, tv), lambda i, j, t: (i, j, 0, 0, 0))
    tok = pl.BlockSpec((1, tq, wd), lambda i, j, t: (i, t, j))
    nbytes = 2 * (hp * tq * V7X_LANES * 2 + 2 * hp * s * V7X_LANES * 2 + tq * wd * 6) \
        + (2 * ATT_LOOKAHEAD + 4) * tq * tk * 4
    return pl.pallas_call(
        functools.partial(_att_kernel, tq=tq, tk=tk, tv=tv, heads=hp),
        grid=(b, ATT_HEADS // hp, s // tq),
        in_specs=[q_tile, k_full, v_full, tok],
        out_specs=tok,
        out_shape=jax.ShapeDtypeStruct((b, s, ATT_WIDTH), BF16),
        scratch_shapes=[pltpu.VMEM((ATT_LOOKAHEAD + 1, tk, tq), F32),
                        pltpu.VMEM((hp, 1, tq), F32),
                        pltpu.VMEM((hp, ATT_VT_ROWS, tq), F32)],
        compiler_params=_params(("parallel", "parallel", "arbitrary"), nbytes),
        name="att",
    )(qa, ka, vt, z_att)


def _ssd_kernel(xbc_ref, z_ref, dt_ref, dtb_ref, alog_ref, dsk_ref, ng_ref, y_ref, state_ref,
                *, chunks):
    L = SSD_CHUNK
    n = SSD_STATE
    gw = SSD_WIDTH // SSD_GROUPS

    @pl.when(pl.program_id(1) == 0)
    def _():
        state_ref[...] = jnp.zeros_like(state_ref)

    row = lax.broadcasted_iota(jnp.int32, (L, L), 0)
    col = lax.broadcasted_iota(jnp.int32, (L, L), 1)
    causal = col <= row
    tri = jnp.where(causal, 1.0, 0.0).astype(BF16)
    lo_half = lax.broadcasted_iota(jnp.int32, (L, V7X_LANES), 1) < SSD_HEAD_DIM
    a_neg = -jnp.exp(alog_ref[...])

    def decay_terms(ci):
        rows = slice(ci * L, (ci + 1) * L)
        dt = _softplus(dt_ref[0, rows, :] + dtb_ref[...])
        acs = _cumsum_rows(dt * a_neg, tri)
        tot = acs[L - 1:L, :]
        return (acs, jnp.exp(acs), jnp.exp(tot), acs.T, dt.T, (dt * jnp.exp(tot - acs)).T)

    terms = decay_terms(0)
    for ci in range(chunks):
        rows = slice(ci * L, (ci + 1) * L)
        acs, e_acs, e_tot, acs_t, dt_t, w_t = terms
        if ci + 1 < chunks:
            terms = decay_terms(ci + 1)

        y_tiles = []
        for g in range(SSD_GROUPS):
            b_lo = SSD_WIDTH + n * g
            c_lo = SSD_WIDTH + SSD_GROUPS * n + n * g
            bg_t = xbc_ref[0, rows, b_lo:b_lo + n].T
            cg = xbc_ref[0, rows, c_lo:c_lo + n].astype(BF16)
            cb = jnp.dot(cg, bg_t.astype(BF16), preferred_element_type=F32)
            s_prev = state_ref[g]
            cs = jnp.dot(cg, s_prev.astype(BF16), preferred_element_type=F32)
            for jj in range(gw // V7X_LANES):
                j = g * (gw // V7X_LANES) + jj
                he, ho = 2 * j, 2 * j + 1
                xp = xbc_ref[0, rows, V7X_LANES * j:V7X_LANES * (j + 1)]
                rhs = jnp.concatenate([jnp.where(lo_half, xp, 0.0), jnp.where(lo_half, 0.0, xp)],
                                      axis=0).astype(BF16)

                def scores(h):
                    seg = acs[:, h:h + 1] - acs_t[h:h + 1, :]
                    return cb * jnp.exp(jnp.where(causal, seg, -jnp.inf)) * dt_t[h:h + 1, :]

                lhs = jnp.concatenate([scores(he), scores(ho)], axis=1).astype(BF16)
                y_diag = jnp.dot(lhs, rhs, preferred_element_type=F32)
                e_pair = jnp.where(lo_half, e_acs[:, he:he + 1], e_acs[:, ho:ho + 1])
                y_off = cs[:, V7X_LANES * jj:V7X_LANES * (jj + 1)] * e_pair
                y_tiles.append(y_diag + y_off + xp * dsk_ref[:, V7X_LANES * j:V7X_LANES * (j + 1)])

                lhs_s = jnp.concatenate([bg_t * w_t[he:he + 1, :], bg_t * w_t[ho:ho + 1, :]],
                                        axis=1).astype(BF16)
                s_new = jnp.dot(lhs_s, rhs, preferred_element_type=F32)
                dec = jnp.where(lo_half[0:1], e_tot[:, he:he + 1], e_tot[:, ho:ho + 1])
                state_ref[g, :, V7X_LANES * jj:V7X_LANES * (jj + 1)] = (
                    s_prev[:, V7X_LANES * jj:V7X_LANES * (jj + 1)] * dec + s_new)

        y = jnp.concatenate(y_tiles, axis=1)
        gated = y * _silu(z_ref[0, rows, :])
        outs = []
        for g in range(SSD_GROUPS):
            gg = gated[:, gw * g:gw * (g + 1)]
            ms = jnp.mean(gg * gg, axis=-1, keepdims=True)
            outs.append(gg * lax.rsqrt(ms + EPS))
        y_ref[0, rows, :] = (jnp.concatenate(outs, axis=1) * ng_ref[...]).astype(y_ref.dtype)


def _ssd(xbc, z, dt_raw, dt_bias, a_log, d_skip, norm_g):
    b, s, _ = xbc.shape
    chunks = SSD_CHUNKS_PER_STEP
    ts = chunks * SSD_CHUNK
    tok = lambda wd: pl.BlockSpec((1, ts, wd), lambda i, t: (i, t, 0))
    par = lambda r, wd: pl.BlockSpec((r, wd), lambda i, t: (0, 0))
    nbytes = 2 * ts * (SSD_CONV_DIM + SSD_WIDTH + V7X_LANES) * 4 + 2 * ts * SSD_WIDTH * 2 \
        + SSD_GROUPS * SSD_STATE * SSD_WIDTH * 2 + 8 * SSD_CHUNK * SSD_CONV_DIM * 4
    return pl.pallas_call(
        functools.partial(_ssd_kernel, chunks=chunks),
        grid=(b, s // ts),
        in_specs=[tok(SSD_CONV_DIM), tok(SSD_WIDTH), tok(V7X_LANES),
                  par(1, V7X_LANES), par(1, V7X_LANES), par(1, SSD_WIDTH), par(1, SSD_WIDTH)],
        out_specs=tok(SSD_WIDTH),
        out_shape=jax.ShapeDtypeStruct((b, s, SSD_WIDTH), BF16),
        scratch_shapes=[pltpu.VMEM((SSD_GROUPS, SSD_STATE, SSD_WIDTH // SSD_GROUPS), F32)],
        compiler_params=_params(("parallel", "arbitrary"), nbytes),
        name="ssd",
    )(xbc, z, dt_raw, dt_bias, a_log, d_skip, norm_g)


def _lru_kernel(x_ref, z_ref, wg_ref, bg_ref, lam_ref, y_ref, a_ref, b_ref, h_ref):
    ts = x_ref.shape[1]
    groups = ts // V7X_SUBLANES
    first_tile = pl.program_id(1) == 0

    @pl.when(first_tile)
    def _():
        h_ref[...] = jnp.zeros_like(h_ref)

    xc = x_ref[0]
    gates = jnp.dot(xc.astype(BF16), wg_ref[...], preferred_element_type=F32) + bg_ref[...]
    r = _sigmoid(gates[:, :LRU_WIDTH])
    i = _sigmoid(gates[:, LRU_WIDTH:])
    log_a = -LRU_C * r * _softplus(-lam_ref[...])
    a = jnp.exp(log_a)
    mult = jnp.sqrt(1.0 - jnp.exp(2.0 * log_a))
    seq_start = first_tile & (lax.broadcasted_iota(jnp.int32, (ts, LRU_WIDTH), 0) == 0)
    mult = jnp.where(seq_start, 1.0, mult)
    b = mult * (i * xc)

    aa = a.reshape(groups, V7X_SUBLANES, LRU_WIDTH)
    bb = b.reshape(groups, V7X_SUBLANES, LRU_WIDTH)
    sub = lax.broadcasted_iota(jnp.int32, (groups, V7X_SUBLANES, LRU_WIDTH), 1)
    for d in (1, 2, 4):
        keep = sub >= d
        bb = jnp.where(keep, aa * pltpu.roll(bb, d, axis=1) + bb, bb)
        aa = jnp.where(keep, aa * pltpu.roll(aa, d, axis=1), aa)
    a_ref[...] = aa.reshape(ts, LRU_WIDTH)
    b_ref[...] = bb.reshape(ts, LRU_WIDTH)

    def group(gi, h):
        r0 = pl.multiple_of(gi * V7X_SUBLANES, V7X_SUBLANES)
        rows = pl.ds(r0, V7X_SUBLANES)
        hh = a_ref[rows, :] * h + b_ref[rows, :]
        b_ref[rows, :] = hh
        return hh[V7X_SUBLANES - 1:V7X_SUBLANES, :]

    h_ref[...] = lax.fori_loop(0, groups, group, h_ref[...], unroll=LRU_UNROLL)
    y_ref[0] = (b_ref[...] * _silu(z_ref[0])).astype(y_ref.dtype)


def _lru(x_lru, z_lru, w_gate, b_gate, lam):
    b, s, _ = x_lru.shape
    ts = TS_LRU
    tok = pl.BlockSpec((1, ts, LRU_WIDTH), lambda i, t: (i, t, 0))
    par = lambda r, wd: pl.BlockSpec((r, wd), lambda i, t: (0, 0))
    nbytes = 2 * 2 * ts * LRU_WIDTH * 4 + 2 * ts * LRU_WIDTH * 2 + 2 * LRU_WIDTH * 2 * LRU_WIDTH * 2 \
        + 2 * ts * LRU_WIDTH * 4 + 8 * ts * LRU_WIDTH * 4
    return pl.pallas_call(
        _lru_kernel,
        grid=(b, s // ts),
        in_specs=[tok, tok, par(LRU_WIDTH, 2 * LRU_WIDTH), par(1, 2 * LRU_WIDTH), par(1, LRU_WIDTH)],
        out_specs=tok,
        out_shape=jax.ShapeDtypeStruct((b, s, LRU_WIDTH), BF16),
        scratch_shapes=[pltpu.VMEM((ts, LRU_WIDTH), F32),
                        pltpu.VMEM((ts, LRU_WIDTH), F32),
                        pltpu.VMEM((1, LRU_WIDTH), F32)],
        compiler_params=_params(("parallel", "arbitrary"), nbytes),
        name="lru",
    )(x_lru, z_lru, w_gate, b_gate, lam)


def _outproj_kernel(h_ref, ya_ref, ys_ref, yl_ref, w_ref, o_ref):
    a0, a1, a2 = ATT_WIDTH, ATT_WIDTH + SSD_WIDTH, MIX_WIDTH
    o_ref[...] = (h_ref[...]
                  + jnp.dot(ya_ref[...], w_ref[0:a0, :], preferred_element_type=F32)
                  + jnp.dot(ys_ref[...], w_ref[a0:a1, :], preferred_element_type=F32)
                  + jnp.dot(yl_ref[...], w_ref[a1:a2, :], preferred_element_type=F32))


def _outproj(h, y_att, y_ssd, y_lru, w):
    t = h.shape[0]
    tm = TM_OUTPROJ
    tok = lambda wd: pl.BlockSpec((tm, wd), lambda i: (i, 0))
    nbytes = 2 * (2 * tm * D_MODEL * 4 + tm * MIX_WIDTH * 2 + MIX_WIDTH * D_MODEL * 2)
    return pl.pallas_call(
        _outproj_kernel,
        grid=(t // tm,),
        in_specs=[tok(D_MODEL), tok(ATT_WIDTH), tok(SSD_WIDTH), tok(LRU_WIDTH),
                  pl.BlockSpec((MIX_WIDTH, D_MODEL), lambda i: (0, 0))],
        out_specs=tok(D_MODEL),
        out_shape=jax.ShapeDtypeStruct((t, D_MODEL), F32),
        compiler_params=_params(("parallel",), nbytes),
        name="outproj",
    )(h, y_att, y_ssd, y_lru, w)


def _pad_lanes(v, width=V7X_LANES):
    return jnp.pad(v, (0, width - v.shape[0]))[None, :]


def _regroup_w_in(w):
    parts = dict(zip(IN_NAMES, jnp.split(w, [sum(IN_SIZES[:i]) for i in range(1, len(IN_SIZES))],
                                         axis=1)))
    cols = [jnp.pad(parts[name], ((0, 0), (0, width - parts[name].shape[1]))) for name, width in _SEGS]
    return jnp.concatenate(cols, axis=1).astype(BF16)


def _block_diag(w):
    nb, d, e = w.shape
    eye = jnp.eye(nb, dtype=w.dtype)
    return jnp.einsum("nde,nm->ndme", w, eye).reshape(nb * d, nb * e)


def kernel(x, norm_g, w_in, q_norm_g, k_norm_g, forget_b, ssd_conv_w, ssd_conv_b, ssd_dt_bias,
           ssd_a_log, ssd_d, ssd_norm_g, lru_conv_w, lru_conv_b, lru_w_a, lru_b_a, lru_w_x, lru_b_x,
           lru_lambda, w_out):
    bsz, seq, d = x.shape
    assert d == D_MODEL and seq % max(TQ_ATT, TM_INPROJ, TS_LRU, SSD_CHUNKS_PER_STEP * SSD_CHUNK) == 0
    assert (bsz * seq) % TM_OUTPROJ == 0
    depth = w_in.shape[0]
    h = x.astype(F32).reshape(bsz * seq, D_MODEL)
    for l in range(depth):
        qa, ka, vt, z_att, z_ssd, xbc, x_lru, z_lru, dt_raw = _inproj(
            h.reshape(bsz, seq, D_MODEL), norm_g[l][None, :], _regroup_w_in(w_in[l]),
            jnp.tile(q_norm_g[l], 2)[None, :], jnp.tile(k_norm_g[l], 2)[None, :],
            _pad_lanes(forget_b[l]), ssd_conv_w[l], ssd_conv_b[l][None, :],
            lru_conv_w[l], lru_conv_b[l][None, :])
        y_att = _attention(qa, ka, vt, z_att)
        y_ssd = _ssd(xbc, z_ssd, dt_raw, _pad_lanes(ssd_dt_bias[l]), _pad_lanes(ssd_a_log[l]),
                     jnp.repeat(ssd_d[l], SSD_HEAD_DIM)[None, :], ssd_norm_g[l][None, :])
        w_gate = jnp.concatenate([_block_diag(lru_w_a[l]), _block_diag(lru_w_x[l])], axis=1).astype(BF16)
        b_gate = jnp.concatenate([lru_b_a[l], lru_b_x[l]])[None, :]
        y_lru = _lru(x_lru, z_lru, w_gate, b_gate, lru_lambda[l][None, :])
        h = _outproj(h, y_att.reshape(bsz * seq, ATT_WIDTH), y_ssd.reshape(bsz * seq, SSD_WIDTH),
                     y_lru.reshape(bsz * seq, LRU_WIDTH), w_out[l].astype(BF16))
    return h.reshape(bsz, seq, D_MODEL).astype(x.dtype)
```

```python
import functools

import jax
import jax.numpy as jnp
from jax import lax
from jax.experimental import pallas as pl
from jax.experimental.pallas import tpu as pltpu

F32 = jnp.float32
BF16 = jnp.bfloat16

V7X_LANES = 128
V7X_SUBLANES = 8
V7X_VMEM_BYTES = 64 * 1024 * 1024

D_MODEL = 1024
EPS = 1e-6
CONV_WIDTH = 4
MIX_WIDTH = 2 * D_MODEL
ATT_HEAD_DIM = 64
ATT_WIDTH = MIX_WIDTH // 4
ATT_HEADS = ATT_WIDTH // ATT_HEAD_DIM
SSD_HEAD_DIM = 64
SSD_WIDTH = MIX_WIDTH // 2
SSD_HEADS = SSD_WIDTH // SSD_HEAD_DIM
SSD_GROUPS = 2
SSD_STATE = 128
SSD_CHUNK = 128
SSD_CONV_DIM = SSD_WIDTH + 2 * SSD_GROUPS * SSD_STATE
LRU_WIDTH = MIX_WIDTH // 4
LRU_BLOCKS = 8
LRU_C = 8.0
IN_SIZES = (ATT_WIDTH, ATT_WIDTH, ATT_WIDTH, ATT_WIDTH, ATT_HEADS,
            SSD_WIDTH, SSD_CONV_DIM, SSD_HEADS, LRU_WIDTH, LRU_WIDTH)
IN_NAMES = ("q", "k", "v", "z_att", "f_raw", "z_ssd", "xbc", "dt_raw", "x_lru", "z_lru")

LOG2E = 1.4426950408889634
HALO = V7X_SUBLANES

SSD_DT_LANE0 = ATT_HEADS
_SEGS = (("q", ATT_WIDTH), ("k", ATT_WIDTH), ("v", ATT_WIDTH), ("z_att", ATT_WIDTH),
         ("z_ssd", SSD_WIDTH), ("xbc", SSD_CONV_DIM), ("x_lru", LRU_WIDTH), ("z_lru", LRU_WIDTH),
         ("fdt", V7X_LANES))
_SEG = {}
for _name, _width in _SEGS:
    _SEG[_name] = (sum(w for _, w in _SEGS[:len(_SEG)]), sum(w for _, w in _SEGS[:len(_SEG)]) + _width)
D_IN_PAD = sum(w for _, w in _SEGS)

TQ_ATT = 512
TK_ATT = 256
ATT_VT_ROWS = 80
ATT_HEADS_PER_STEP = 8
ATT_LOOKAHEAD = 3
TM_INPROJ = 256
INPROJ_CHUNK = 256
SSD_CHUNKS_PER_STEP = 2
TS_LRU = 1024
LRU_UNROLL = 4
TM_OUTPROJ = 1024


def _vmem_limit(nbytes):
    return int(min(nbytes * 3 // 2 + (8 << 20), V7X_VMEM_BYTES - (6 << 20)))


def _params(semantics, nbytes):
    return pltpu.CompilerParams(dimension_semantics=semantics, vmem_limit_bytes=_vmem_limit(nbytes))


def _sigmoid(x):
    return 1.0 / (1.0 + jnp.exp(-x))


def _silu(x):
    return x * _sigmoid(x)


def _softplus(x):
    return jnp.maximum(x, 0.0) + jnp.log1p(jnp.exp(-jnp.abs(x)))


def _split3(x):
    hi = x.astype(BF16)
    r1 = x - hi.astype(F32)
    mid = r1.astype(BF16)
    lo = (r1 - mid.astype(F32)).astype(BF16)
    return hi, mid, lo


def _tril_ones(n):
    row = lax.broadcasted_iota(jnp.int32, (n, n), 0)
    col = lax.broadcasted_iota(jnp.int32, (n, n), 1)
    return jnp.where(col <= row, 1.0, 0.0).astype(BF16)


def _cumsum_rows(x, tri):
    hi, mid, lo = _split3(x)
    dot = lambda a: jnp.dot(tri, a, preferred_element_type=F32)
    return dot(hi) + dot(mid) + dot(lo)


def _causal_conv(pad_ref, w_ref, b_ref, cols):
    rows = pad_ref.shape[0] - HALO
    xp = pad_ref[:, cols]
    x = xp[HALO:, :]
    y = b_ref[:, cols] + w_ref[CONV_WIDTH - 1:CONV_WIDTH, cols] * x
    for shift in range(1, CONV_WIDTH):
        kk = CONV_WIDTH - 1 - shift
        y = y + w_ref[kk:kk + 1, cols] * pltpu.roll(xp, shift, axis=0)[HALO:, :]
    pad_ref[0:HALO, cols] = x[rows - HALO:rows, :]
    return y


def _forget_bias_terms(qkvf_ref, fb, cum_ref, bias_ref):
    ts = qkvf_ref.shape[0]
    log_f = -_softplus(-(qkvf_ref[:, 3 * ATT_WIDTH:] + fb))
    c = cum_ref[...] + _cumsum_rows(log_f, _tril_ones(ts))
    cum_ref[...] = c[ts - 1:ts, :]
    for t, term in enumerate(_split3(c * (-LOG2E))):
        bias_ref[t] = term.astype(F32)


def _att_operands_pair(j, qkvf_ref, bias_ref, qg, kg, qa_ref, ka_ref, vt_ref):
    ts = qkvf_ref.shape[0]
    lane = lax.broadcasted_iota(jnp.int32, (ts, V7X_LANES), 1)
    lo_half = lane < ATT_HEAD_DIM
    ones3 = jnp.where((lane >= ATT_HEAD_DIM) & (lane < ATT_HEAD_DIM + 3), 1.0, 0.0)
    q_scale = ATT_HEAD_DIM ** -0.5 * LOG2E

    def normed(x, g):
        sq = x * x
        ss_lo = jnp.sum(jnp.where(lo_half, sq, 0.0), axis=1, keepdims=True)
        ss_hi = jnp.sum(jnp.where(lo_half, 0.0, sq), axis=1, keepdims=True)
        inv = jnp.where(lo_half, lax.rsqrt(ss_lo / ATT_HEAD_DIM + EPS),
                        lax.rsqrt(ss_hi / ATT_HEAD_DIM + EPS))
        return x * inv * g

    lo, hi = V7X_LANES * j, V7X_LANES * (j + 1)
    qn = normed(qkvf_ref[:, lo:hi], qg) * q_scale
    kn = normed(qkvf_ref[:, ATT_WIDTH + lo:ATT_WIDTH + hi], kg)
    vp_t = qkvf_ref[:, 2 * ATT_WIDTH + lo:2 * ATT_WIDTH + hi].T
    pad_rows = ATT_VT_ROWS - ATT_HEAD_DIM
    ones_row = jnp.where(lax.broadcasted_iota(jnp.int32, (pad_rows, ts), 0) == 0, 1.0, 0.0)
    vt_ref[0, 2 * j, 0] = jnp.concatenate([vp_t[:ATT_HEAD_DIM, :], ones_row], axis=0).astype(BF16)
    vt_ref[0, 2 * j + 1, 0] = jnp.concatenate([ones_row, vp_t[ATT_HEAD_DIM:, :]], axis=0).astype(BF16)
    for parity in range(2):
        h = 2 * j + parity
        qh = qn if parity == 0 else pltpu.roll(qn, ATT_HEAD_DIM, axis=1)
        kh = kn if parity == 0 else pltpu.roll(kn, ATT_HEAD_DIM, axis=1)
        bias = jnp.where(lane == ATT_HEAD_DIM, bias_ref[0, :, h:h + 1],
                         jnp.where(lane == ATT_HEAD_DIM + 1, bias_ref[1, :, h:h + 1],
                                   jnp.where(lane == ATT_HEAD_DIM + 2, bias_ref[2, :, h:h + 1], 0.0)))
        qa_ref[0, h] = jnp.where(lo_half, qh, ones3).astype(BF16)
        ka_ref[0, h] = jnp.where(lo_half, kh, bias).astype(BF16)


def _inproj_kernel(x_ref, g_ref, w_ref, qg_ref, kg_ref, fb_ref, scw_ref, scb_ref, lcw_ref, lcb_ref,
                   qa_ref, ka_ref, vt_ref, zatt_ref, zssd_ref, xbc_ref, xlru_ref, zlru_ref, dt_ref,
                   cum_ref, spad_ref, lpad_ref, qkvf_ref, bias_ref):
    tm = x_ref.shape[1]

    @pl.when(pl.program_id(1) == 0)
    def _():
        cum_ref[...] = jnp.zeros_like(cum_ref)
        spad_ref[0:HALO, :] = jnp.zeros((HALO, SSD_CONV_DIM), F32)
        lpad_ref[0:HALO, :] = jnp.zeros((HALO, LRU_WIDTH), F32)

    x = x_ref[0]
    ms = jnp.mean(x * x, axis=-1, keepdims=True)
    u = (x * lax.rsqrt(ms + EPS) * g_ref[...]).astype(BF16)

    def proj(name, cols=None):
        lo, hi = _SEG[name]
        if cols is not None:
            lo, hi = lo + cols.start, lo + cols.stop
        return jnp.dot(u, w_ref[:, lo:hi], preferred_element_type=F32)

    chunk = lambda c: slice(c * INPROJ_CHUNK, (c + 1) * INPROJ_CHUNK)
    plain = [(ref, name, chunk(c))
             for ref, name, width in ((zssd_ref, "z_ssd", SSD_WIDTH), (zatt_ref, "z_att", ATT_WIDTH),
                                      (zlru_ref, "z_lru", LRU_WIDTH))
             for c in range(width // INPROJ_CHUNK)]
    convs = [(spad_ref, scw_ref, scb_ref, xbc_ref, "xbc", chunk(c), True)
             for c in range(SSD_CONV_DIM // INPROJ_CHUNK)]
    convs += [(lpad_ref, lcw_ref, lcb_ref, xlru_ref, "x_lru", chunk(c), False)
              for c in range(LRU_WIDTH // INPROJ_CHUNK)]

    def matmul_plain():
        if plain:
            ref, name, cols = plain.pop(0)
            ref[0, :, cols] = proj(name, cols)

    def matmul_conv(ci):
        if ci < len(convs):
            pad_ref, _, _, _, name, cols, _ = convs[ci]
            pad_ref[HALO:HALO + tm, cols] = proj(name, cols)

    def finish_conv(ci):
        pad_ref, cw_ref, cb_ref, out_ref, _, cols, silu = convs[ci]
        y = _causal_conv(pad_ref, cw_ref, cb_ref, cols)
        out_ref[0, :, cols] = _silu(y) if silu else y

    for si, name in enumerate(("q", "k", "v")):
        qkvf_ref[:, si * ATT_WIDTH:(si + 1) * ATT_WIDTH] = proj(name)
    fdt = proj("fdt")
    qkvf_ref[:, 3 * ATT_WIDTH:] = fdt
    dt_ref[0] = fdt
    matmul_conv(0)
    matmul_plain()
    _forget_bias_terms(qkvf_ref, fb_ref[...], cum_ref, bias_ref)
    pairs = list(range(ATT_HEADS // 2))
    for ci in range(len(convs)):
        matmul_conv(ci + 1)
        if pairs:
            _att_operands_pair(pairs.pop(0), qkvf_ref, bias_ref, qg_ref[...], kg_ref[...],
                               qa_ref, ka_ref, vt_ref)
        matmul_plain()
        finish_conv(ci)
    while plain:
        matmul_plain()
    assert not pairs


def _inproj(h, g, w, qg, kg, fb, ssd_cw, ssd_cb, lru_cw, lru_cb):
    b, s, _ = h.shape
    tm = TM_INPROJ
    tok = lambda wd: pl.BlockSpec((1, tm, wd), lambda i, t: (i, t, 0))
    par = lambda r, wd: pl.BlockSpec((r, wd), lambda i, t: (0, 0))
    head = pl.BlockSpec((1, ATT_HEADS, tm, V7X_LANES), lambda i, t: (i, 0, t, 0))
    head_t = pl.BlockSpec((1, ATT_HEADS, 1, ATT_VT_ROWS, tm), lambda i, t: (i, 0, t, 0, 0))
    tok_widths = (ATT_WIDTH, SSD_WIDTH, SSD_CONV_DIM, LRU_WIDTH, LRU_WIDTH, V7X_LANES)
    nbytes = 2 * (tm * D_MODEL * 4 + D_MODEL * D_IN_PAD * 2 + tm * sum(tok_widths) * 4
                  + 3 * ATT_HEADS * tm * V7X_LANES * 2) \
        + (HALO + tm) * (SSD_CONV_DIM + LRU_WIDTH) * 4 + 4 * tm * SSD_CONV_DIM * 4
    return pl.pallas_call(
        _inproj_kernel,
        grid=(b, s // tm),
        in_specs=[tok(D_MODEL), par(1, D_MODEL), par(D_MODEL, D_IN_PAD),
                  par(1, V7X_LANES), par(1, V7X_LANES), par(1, V7X_LANES),
                  par(CONV_WIDTH, SSD_CONV_DIM), par(1, SSD_CONV_DIM),
                  par(CONV_WIDTH, LRU_WIDTH), par(1, LRU_WIDTH)],
        out_specs=[head, head, head_t] + [tok(wd) for wd in tok_widths],
        out_shape=[jax.ShapeDtypeStruct((b, ATT_HEADS, s, V7X_LANES), BF16),
                   jax.ShapeDtypeStruct((b, ATT_HEADS, s, V7X_LANES), BF16),
                   jax.ShapeDtypeStruct((b, ATT_HEADS, s // tm, ATT_VT_ROWS, tm), BF16)]
        + [jax.ShapeDtypeStruct((b, s, wd), F32) for wd in tok_widths],
        scratch_shapes=[pltpu.VMEM((1, V7X_LANES), F32),
                        pltpu.VMEM((HALO + tm, SSD_CONV_DIM), F32),
                        pltpu.VMEM((HALO + tm, LRU_WIDTH), F32),
                        pltpu.VMEM((tm, 3 * ATT_WIDTH + V7X_LANES), F32),
                        pltpu.VMEM((3, tm, V7X_LANES), F32)],
        compiler_params=_params(("parallel", "arbitrary"), nbytes),
        name="inproj",
    )(h, g, w, qg, kg, fb, ssd_cw, ssd_cb, lru_cw, lru_cb)


def _att_kernel(qa_ref, ka_ref, vt_ref, z_ref, o_ref, s_ref, cmax_ref, m_ref, acc_ref,
                *, tq, tk, tv, heads):
    qi = pl.program_id(2)
    subs = tq // tk
    nt = (((1,), (1,)), ((), ()))
    chains = [(sub, hh) for sub in range(subs) for hh in range(heads)]
    slots = ATT_LOOKAHEAD + 1

    m_ref[...] = jnp.full(m_ref.shape, -jnp.inf, F32)
    acc_ref[...] = jnp.zeros(acc_ref.shape, F32)

    assert len(chains) % slots == 0 and ATT_LOOKAHEAD <= heads

    def causal_mask(s_t, sub, q0):
        key_r = lax.broadcasted_iota(jnp.int32, s_t.shape, 0)
        qry_c = lax.broadcasted_iota(jnp.int32, s_t.shape, 1)
        return jnp.where(key_r + (sub * tk - q0) <= qry_c, s_t, -jnp.inf)

    def issue_logits(j, idx, q0=0, diagonal=False):
        sub, hh = chains[idx]
        k0 = pl.multiple_of((j * subs + sub) * tk, tk)
        s_t = lax.dot_general(ka_ref[0, hh, pl.ds(k0, tk), :], qa_ref[0, hh, q0:, :], nt,
                              preferred_element_type=F32)
        if diagonal:
            s_t = causal_mask(s_t, sub, q0)
        s_ref[idx % slots, :, q0:] = s_t
        cmax_ref[idx % slots, :, q0:] = jnp.max(s_t, axis=0, keepdims=True)

    def step(j, diagonal):
        for idx, (sub, hh) in enumerate(chains):
            q0 = sub * tk if diagonal else 0
            s_t = s_ref[idx % slots, :, q0:]
            cmax = cmax_ref[idx % slots, :, q0:]
            ahead = idx + ATT_LOOKAHEAD
            if ahead < len(chains):
                issue_logits(j, ahead, chains[ahead][0] * tk if diagonal else 0, diagonal)
            elif not diagonal:
                issue_logits(j + 1, ahead - len(chains))
            if diagonal and idx < ATT_LOOKAHEAD:
                s_t = causal_mask(s_t, sub, q0)
                cmax = jnp.max(s_t, axis=0, keepdims=True)
            m = m_ref[hh, :, q0:]
            m_new = jnp.maximum(m, cmax)
            m_ref[hh, :, q0:] = m_new
            p_t = jnp.exp2(s_t - m_new).astype(BF16)
            acc = acc_ref[hh, :, q0:] * jnp.exp2(m - m_new)
            for vb in range(tk // tv):
                acc = acc + jnp.dot(vt_ref[0, hh, (j * subs + sub) * (tk // tv) + vb],
                                    p_t[vb * tv:(vb + 1) * tv, :], preferred_element_type=F32)
            acc_ref[hh, :, q0:] = acc

    for idx in range(ATT_LOOKAHEAD):
        issue_logits(0, idx)

    @pl.loop(0, qi)
    def _(j):
        step(j, False)

    step(qi, True)

    lead = ATT_VT_ROWS - ATT_HEAD_DIM
    for pr in range(heads // 2):
        even, odd = acc_ref[2 * pr], acc_ref[2 * pr + 1]
        o_t = jnp.concatenate(
            [even[0:ATT_HEAD_DIM, :] * (1.0 / even[ATT_HEAD_DIM:ATT_HEAD_DIM + 1, :]),
             odd[lead:, :] * (1.0 / odd[0:1, :])], axis=0)
        lanes = slice(V7X_LANES * pr, V7X_LANES * (pr + 1))
        o_ref[0, :, lanes] = (o_t.T * _silu(z_ref[0, :, lanes])).astype(o_ref.dtype)


def _attention(qa, ka, vt, z_att):
    b, _, s, _ = qa.shape
    tq, tk, hp = TQ_ATT, TK_ATT, ATT_HEADS_PER_STEP
    assert tq % tk == 0 and hp % 2 == 0 and ATT_HEADS % hp == 0
    wd = hp * ATT_HEAD_DIM
    q_tile = pl.BlockSpec((1, hp, tq, V7X_LANES), lambda i, j, t: (i, j, t, 0))
    k_full = pl.BlockSpec((1, hp, s, V7X_LANES), lambda i, j, t: (i, j, 0, 0))
    tv = vt.shape[-1]
    assert tk % tv == 0
    v_full = pl.BlockSpec((1, hp, s // tv, ATT_VT_ROWS, tv), lambda i, j, t: (i, j, 0, 0, 0))
    tok = pl.BlockSpec((1, tq, wd), lambda i, j, t: (i, t, j))
    nbytes = 2 * (hp * tq * V7X_LANES * 2 + 2 * hp * s * V7X_LANES * 2 + tq * wd * 6) \
        + (2 * ATT_LOOKAHEAD + 4) * tq * tk * 4
    return pl.pallas_call(
        functools.partial(_att_kernel, tq=tq, tk=tk, tv=tv, heads=hp),
        grid=(b, ATT_HEADS // hp, s // tq),
        in_specs=[q_tile, k_full, v_full, tok],
        out_specs=tok,
        out_shape=jax.ShapeDtypeStruct((b, s, ATT_WIDTH), BF16),
        scratch_shapes=[pltpu.VMEM((ATT_LOOKAHEAD + 1, tk, tq), F32),
                        pltpu.VMEM((ATT_LOOKAHEAD + 1, 1, tq), F32),
                        pltpu.VMEM((hp, 1, tq), F32),
                        pltpu.VMEM((hp, ATT_VT_ROWS, tq), F32)],
        compiler_params=_params(("parallel", "parallel", "arbitrary"), nbytes),
        name="att",
    )(qa, ka, vt, z_att)


def _ssd_kernel(xbc_ref, z_ref, dt_ref, dtb_ref, alog_ref, dsk_ref, ng_ref, y_ref, state_ref,
                *, chunks):
    L = SSD_CHUNK
    n = SSD_STATE
    gw = SSD_WIDTH // SSD_GROUPS

    @pl.when(pl.program_id(1) == 0)
    def _():
        state_ref[...] = jnp.zeros_like(state_ref)

    row = lax.broadcasted_iota(jnp.int32, (L, L), 0)
    col = lax.broadcasted_iota(jnp.int32, (L, L), 1)
    causal = col <= row
    tri = jnp.where(causal, 1.0, 0.0).astype(BF16)
    lo_half = lax.broadcasted_iota(jnp.int32, (L, V7X_LANES), 1) < SSD_HEAD_DIM
    a_neg = -jnp.exp(alog_ref[...])

    def decay_terms(ci):
        rows = slice(ci * L, (ci + 1) * L)
        dt = _softplus(dt_ref[0, rows, :] + dtb_ref[...])
        acs = _cumsum_rows(dt * a_neg, tri)
        tot = acs[L - 1:L, :]
        return (acs, jnp.exp(acs), jnp.exp(tot), acs.T, dt.T, (dt * jnp.exp(tot - acs)).T)

    terms = decay_terms(0)
    for ci in range(chunks):
        rows = slice(ci * L, (ci + 1) * L)
        acs, e_acs, e_tot, acs_t, dt_t, w_t = terms
        if ci + 1 < chunks:
            terms = decay_terms(ci + 1)

        y_tiles = []
        for g in range(SSD_GROUPS):
            b_lo = SSD_WIDTH + n * g
            c_lo = SSD_WIDTH + SSD_GROUPS * n + n * g
            bg_t = xbc_ref[0, rows, b_lo:b_lo + n].T
            cg = xbc_ref[0, rows, c_lo:c_lo + n].astype(BF16)
            cb = jnp.dot(cg, bg_t.astype(BF16), preferred_element_type=F32)
            s_prev = state_ref[g]
            cs = jnp.dot(cg, s_prev.astype(BF16), preferred_element_type=F32)
            for jj in range(gw // V7X_LANES):
                j = g * (gw // V7X_LANES) + jj
                he, ho = SSD_DT_LANE0 + 2 * j, SSD_DT_LANE0 + 2 * j + 1
                xp = xbc_ref[0, rows, V7X_LANES * j:V7X_LANES * (j + 1)]
                rhs = jnp.concatenate([jnp.where(lo_half, xp, 0.0), jnp.where(lo_half, 0.0, xp)],
                                      axis=0).astype(BF16)

                def scores(h):
                    seg = acs[:, h:h + 1] - acs_t[h:h + 1, :]
                    return cb * jnp.exp(jnp.where(causal, seg, -jnp.inf)) * dt_t[h:h + 1, :]

                lhs = jnp.concatenate([scores(he), scores(ho)], axis=1).astype(BF16)
                y_diag = jnp.dot(lhs, rhs, preferred_element_type=F32)
                e_pair = jnp.where(lo_half, e_acs[:, he:he + 1], e_acs[:, ho:ho + 1])
                y_off = cs[:, V7X_LANES * jj:V7X_LANES * (jj + 1)] * e_pair
                y_tiles.append(y_diag + y_off + xp * dsk_ref[:, V7X_LANES * j:V7X_LANES * (j + 1)])

                lhs_s = jnp.concatenate([bg_t * w_t[he:he + 1, :], bg_t * w_t[ho:ho + 1, :]],
                                        axis=1).astype(BF16)
                s_new = jnp.dot(lhs_s, rhs, preferred_element_type=F32)
                dec = jnp.where(lo_half[0:1], e_tot[:, he:he + 1], e_tot[:, ho:ho + 1])
                state_ref[g, :, V7X_LANES * jj:V7X_LANES * (jj + 1)] = (
                    s_prev[:, V7X_LANES * jj:V7X_LANES * (jj + 1)] * dec + s_new)

        y = jnp.concatenate(y_tiles, axis=1)
        gated = y * _silu(z_ref[0, rows, :])
        outs = []
        for g in range(SSD_GROUPS):
            gg = gated[:, gw * g:gw * (g + 1)]
            ms = jnp.mean(gg * gg, axis=-1, keepdims=True)
            outs.append(gg * lax.rsqrt(ms + EPS))
        y_ref[0, rows, :] = (jnp.concatenate(outs, axis=1) * ng_ref[...]).astype(y_ref.dtype)


def _ssd(xbc, z, dt_raw, dt_bias, a_log, d_skip, norm_g):
    b, s, _ = xbc.shape
    chunks = SSD_CHUNKS_PER_STEP
    ts = chunks * SSD_CHUNK
    tok = lambda wd: pl.BlockSpec((1, ts, wd), lambda i, t: (i, t, 0))
    par = lambda r, wd: pl.BlockSpec((r, wd), lambda i, t: (0, 0))
    nbytes = 2 * ts * (SSD_CONV_DIM + SSD_WIDTH + V7X_LANES) * 4 + 2 * ts * SSD_WIDTH * 2 \
        + SSD_GROUPS * SSD_STATE * SSD_WIDTH * 2 + 8 * SSD_CHUNK * SSD_CONV_DIM * 4
    return pl.pallas_call(
        functools.partial(_ssd_kernel, chunks=chunks),
        grid=(b, s // ts),
        in_specs=[tok(SSD_CONV_DIM), tok(SSD_WIDTH), tok(V7X_LANES),
                  par(1, V7X_LANES), par(1, V7X_LANES), par(1, SSD_WIDTH), par(1, SSD_WIDTH)],
        out_specs=tok(SSD_WIDTH),
        out_shape=jax.ShapeDtypeStruct((b, s, SSD_WIDTH), BF16),
        scratch_shapes=[pltpu.VMEM((SSD_GROUPS, SSD_STATE, SSD_WIDTH // SSD_GROUPS), F32)],
        compiler_params=_params(("parallel", "arbitrary"), nbytes),
        name="ssd",
    )(xbc, z, dt_raw, dt_bias, a_log, d_skip, norm_g)


def _lru_kernel(x_ref, z_ref, wg_ref, bg_ref, lam_ref, y_ref, a_ref, b_ref, h_ref):
    ts = x_ref.shape[1]
    groups = ts // V7X_SUBLANES
    first_tile = pl.program_id(1) == 0

    @pl.when(first_tile)
    def _():
        h_ref[...] = jnp.zeros_like(h_ref)

    xc = x_ref[0]
    gates = jnp.dot(xc.astype(BF16), wg_ref[...], preferred_element_type=F32) + bg_ref[...]
    r = _sigmoid(gates[:, :LRU_WIDTH])
    i = _sigmoid(gates[:, LRU_WIDTH:])
    log_a = -LRU_C * r * _softplus(-lam_ref[...])
    a = jnp.exp(log_a)
    mult = jnp.sqrt(1.0 - jnp.exp(2.0 * log_a))
    seq_start = first_tile & (lax.broadcasted_iota(jnp.int32, (ts, LRU_WIDTH), 0) == 0)
    mult = jnp.where(seq_start, 1.0, mult)
    b = mult * (i * xc)

    aa = a.reshape(groups, V7X_SUBLANES, LRU_WIDTH)
    bb = b.reshape(groups, V7X_SUBLANES, LRU_WIDTH)
    sub = lax.broadcasted_iota(jnp.int32, (groups, V7X_SUBLANES, LRU_WIDTH), 1)
    for d in (1, 2, 4):
        keep = sub >= d
        bb = jnp.where(keep, aa * pltpu.roll(bb, d, axis=1) + bb, bb)
        aa = jnp.where(keep, aa * pltpu.roll(aa, d, axis=1), aa)
    a_ref[...] = aa.reshape(ts, LRU_WIDTH)
    b_ref[...] = bb.reshape(ts, LRU_WIDTH)

    def group(gi, h):
        r0 = pl.multiple_of(gi * V7X_SUBLANES, V7X_SUBLANES)
        rows = pl.ds(r0, V7X_SUBLANES)
        hh = a_ref[rows, :] * h + b_ref[rows, :]
        b_ref[rows, :] = hh
        return hh[V7X_SUBLANES - 1:V7X_SUBLANES, :]

    h_ref[...] = lax.fori_loop(0, groups, group, h_ref[...], unroll=LRU_UNROLL)
    y_ref[0] = (b_ref[...] * _silu(z_ref[0])).astype(y_ref.dtype)


def _lru(x_lru, z_lru, w_gate, b_gate, lam):
    b, s, _ = x_lru.shape
    ts = TS_LRU
    tok = pl.BlockSpec((1, ts, LRU_WIDTH), lambda i, t: (i, t, 0))
    par = lambda r, wd: pl.BlockSpec((r, wd), lambda i, t: (0, 0))
    nbytes = 2 * 2 * ts * LRU_WIDTH * 4 + 2 * ts * LRU_WIDTH * 2 + 2 * LRU_WIDTH * 2 * LRU_WIDTH * 2 \
        + 2 * ts * LRU_WIDTH * 4 + 8 * ts * LRU_WIDTH * 4
    return pl.pallas_call(
        _lru_kernel,
        grid=(b, s // ts),
        in_specs=[tok, tok, par(LRU_WIDTH, 2 * LRU_WIDTH), par(1, 2 * LRU_WIDTH), par(1, LRU_WIDTH)],
        out_specs=tok,
        out_shape=jax.ShapeDtypeStruct((b, s, LRU_WIDTH), BF16),
        scratch_shapes=[pltpu.VMEM((ts, LRU_WIDTH), F32),
                        pltpu.VMEM((ts, LRU_WIDTH), F32),
                        pltpu.VMEM((1, LRU_WIDTH), F32)],
        compiler_params=_params(("parallel", "arbitrary"), nbytes),
        name="lru",
    )(x_lru, z_lru, w_gate, b_gate, lam)


def _outproj_kernel(h_ref, ya_ref, ys_ref, yl_ref, w_ref, o_ref):
    a0, a1, a2 = ATT_WIDTH, ATT_WIDTH + SSD_WIDTH, MIX_WIDTH
    o_ref[...] = (h_ref[...]
                  + jnp.dot(ya_ref[...], w_ref[0:a0, :], preferred_element_type=F32)
                  + jnp.dot(ys_ref[...], w_ref[a0:a1, :], preferred_element_type=F32)
                  + jnp.dot(yl_ref[...], w_ref[a1:a2, :], preferred_element_type=F32))


def _outproj(h, y_att, y_ssd, y_lru, w):
    t = h.shape[0]
    tm = TM_OUTPROJ
    tok = lambda wd: pl.BlockSpec((tm, wd), lambda i: (i, 0))
    nbytes = 2 * (2 * tm * D_MODEL * 4 + tm * MIX_WIDTH * 2 + MIX_WIDTH * D_MODEL * 2)
    return pl.pallas_call(
        _outproj_kernel,
        grid=(t // tm,),
        in_specs=[tok(D_MODEL), tok(ATT_WIDTH), tok(SSD_WIDTH), tok(LRU_WIDTH),
                  pl.BlockSpec((MIX_WIDTH, D_MODEL), lambda i: (0, 0))],
        out_specs=tok(D_MODEL),
        out_shape=jax.ShapeDtypeStruct((t, D_MODEL), F32),
        compiler_params=_params(("parallel",), nbytes),
        name="outproj",
    )(h, y_att, y_ssd, y_lru, w)


def _pad_lanes(v, lane0=0, width=V7X_LANES):
    return jnp.pad(v, (lane0, width - lane0 - v.shape[0]))[None, :]


def _regroup_w_in(w):
    parts = dict(zip(IN_NAMES, jnp.split(w, [sum(IN_SIZES[:i]) for i in range(1, len(IN_SIZES))],
                                         axis=1)))
    parts["fdt"] = jnp.concatenate([parts["f_raw"], parts["dt_raw"]], axis=1)
    cols = [jnp.pad(parts[name], ((0, 0), (0, width - parts[name].shape[1]))) for name, width in _SEGS]
    return jnp.concatenate(cols, axis=1).astype(BF16)


def _block_diag(w):
    nb, d, e = w.shape
    eye = jnp.eye(nb, dtype=w.dtype)
    return jnp.einsum("nde,nm->ndme", w, eye).reshape(nb * d, nb * e)


def kernel(x, norm_g, w_in, q_norm_g, k_norm_g, forget_b, ssd_conv_w, ssd_conv_b, ssd_dt_bias,
           ssd_a_log, ssd_d, ssd_norm_g, lru_conv_w, lru_conv_b, lru_w_a, lru_b_a, lru_w_x, lru_b_x,
           lru_lambda, w_out):
    bsz, seq, d = x.shape
    assert d == D_MODEL and seq % max(TQ_ATT, TM_INPROJ, TS_LRU, SSD_CHUNKS_PER_STEP * SSD_CHUNK) == 0
    assert (bsz * seq) % TM_OUTPROJ == 0
    depth = w_in.shape[0]
    h = x.astype(F32).reshape(bsz * seq, D_MODEL)
    for l in range(depth):
        qa, ka, vt, z_att, z_ssd, xbc, x_lru, z_lru, dt_raw = _inproj(
            h.reshape(bsz, seq, D_MODEL), norm_g[l][None, :], _regroup_w_in(w_in[l]),
            jnp.tile(q_norm_g[l], 2)[None, :], jnp.tile(k_norm_g[l], 2)[None, :],
            _pad_lanes(forget_b[l]), ssd_conv_w[l], ssd_conv_b[l][None, :],
            lru_conv_w[l], lru_conv_b[l][None, :])
        y_att = _attention(qa, ka, vt, z_att)
        y_ssd = _ssd(xbc, z_ssd, dt_raw, _pad_lanes(ssd_dt_bias[l], SSD_DT_LANE0),
                     _pad_lanes(ssd_a_log[l], SSD_DT_LANE0),
                     jnp.repeat(ssd_d[l], SSD_HEAD_DIM)[None, :], ssd_norm_g[l][None, :])
        w_gate = jnp.concatenate([_block_diag(lru_w_a[l]), _block_diag(lru_w_x[l])], axis=1).astype(BF16)
        b_gate = jnp.concatenate([lru_b_a[l], lru_b_x[l]])[None, :]
        y_lru = _lru(x_lru, z_lru, w_gate, b_gate, lru_lambda[l][None, :])
        h = _outproj(h, y_att.reshape(bsz * seq, ATT_WIDTH), y_ssd.reshape(bsz * seq, SSD_WIDTH),
                     y_lru.reshape(bsz * seq, LRU_WIDTH), w_out[l].astype(BF16))
    return h.reshape(bsz, seq, D_MODEL).astype(x.dtype)
```

```python
import functools

import jax
import jax.numpy as jnp
from jax import lax
from jax.experimental import pallas as pl
from jax.experimental.pallas import tpu as pltpu

F32 = jnp.float32
BF16 = jnp.bfloat16

V7X_LANES = 128
V7X_SUBLANES = 8
V7X_VMEM_BYTES = 64 * 1024 * 1024

D_MODEL = 1024
EPS = 1e-6
CONV_WIDTH = 4
MIX_WIDTH = 2 * D_MODEL
ATT_HEAD_DIM = 64
ATT_WIDTH = MIX_WIDTH // 4
ATT_HEADS = ATT_WIDTH // ATT_HEAD_DIM
SSD_HEAD_DIM = 64
SSD_WIDTH = MIX_WIDTH // 2
SSD_HEADS = SSD_WIDTH // SSD_HEAD_DIM
SSD_GROUPS = 2
SSD_STATE = 128
SSD_CHUNK = 128
SSD_CONV_DIM = SSD_WIDTH + 2 * SSD_GROUPS * SSD_STATE
LRU_WIDTH = MIX_WIDTH // 4
LRU_BLOCKS = 8
LRU_C = 8.0
IN_SIZES = (ATT_WIDTH, ATT_WIDTH, ATT_WIDTH, ATT_WIDTH, ATT_HEADS,
            SSD_WIDTH, SSD_CONV_DIM, SSD_HEADS, LRU_WIDTH, LRU_WIDTH)
IN_NAMES = ("q", "k", "v", "z_att", "f_raw", "z_ssd", "xbc", "dt_raw", "x_lru", "z_lru")

LOG2E = 1.4426950408889634
HALO = V7X_SUBLANES

SSD_DT_LANE0 = ATT_HEADS
_SEGS = (("q", ATT_WIDTH), ("k", ATT_WIDTH), ("v", ATT_WIDTH), ("z_att", ATT_WIDTH),
         ("z_ssd", SSD_WIDTH), ("xbc", SSD_CONV_DIM), ("x_lru", LRU_WIDTH), ("z_lru", LRU_WIDTH),
         ("fdt", V7X_LANES))
_SEG = {}
for _name, _width in _SEGS:
    _SEG[_name] = (sum(w for _, w in _SEGS[:len(_SEG)]), sum(w for _, w in _SEGS[:len(_SEG)]) + _width)
D_IN_PAD = sum(w for _, w in _SEGS)

TQ_ATT = 512
TK_ATT = 256
ATT_VT_ROWS = 80
ATT_HEADS_PER_STEP = 8
ATT_LOOKAHEAD = 3
TM_INPROJ = 256
INPROJ_CHUNK = 256
SSD_CHUNKS_PER_STEP = 2
TS_LRU = 1024
LRU_UNROLL = 4
TM_OUTPROJ = 1024


def _vmem_limit(nbytes):
    return int(min(nbytes * 3 // 2 + (8 << 20), V7X_VMEM_BYTES - (6 << 20)))


def _params(semantics, nbytes):
    return pltpu.CompilerParams(dimension_semantics=semantics, vmem_limit_bytes=_vmem_limit(nbytes))


def _sigmoid(x):
    return 1.0 / (1.0 + jnp.exp(-x))


def _silu(x):
    return x * _sigmoid(x)


def _softplus(x):
    return jnp.maximum(x, 0.0) + jnp.log(1.0 + jnp.exp(-jnp.abs(x)))


def _split3(x):
    hi = x.astype(BF16)
    r1 = x - hi.astype(F32)
    mid = r1.astype(BF16)
    lo = (r1 - mid.astype(F32)).astype(BF16)
    return hi, mid, lo


def _tril_ones(n):
    row = lax.broadcasted_iota(jnp.int32, (n, n), 0)
    col = lax.broadcasted_iota(jnp.int32, (n, n), 1)
    return jnp.where(col <= row, 1.0, 0.0).astype(BF16)


def _cumsum_rows(x, tri):
    hi, mid, lo = _split3(x)
    dot = lambda a: jnp.dot(tri, a, preferred_element_type=F32)
    return dot(hi) + dot(mid) + dot(lo)


def _causal_conv(pad_ref, w_ref, b_ref, cols):
    rows = pad_ref.shape[0] - HALO
    xp = pad_ref[:, cols]
    x = xp[HALO:, :]
    y = b_ref[:, cols] + w_ref[CONV_WIDTH - 1:CONV_WIDTH, cols] * x
    for shift in range(1, CONV_WIDTH):
        kk = CONV_WIDTH - 1 - shift
        y = y + w_ref[kk:kk + 1, cols] * pltpu.roll(xp, shift, axis=0)[HALO:, :]
    pad_ref[0:HALO, cols] = x[rows - HALO:rows, :]
    return y


def _forget_bias_terms(qkvf_ref, fb, cum_ref, bias_ref):
    ts = qkvf_ref.shape[0]
    log_f = -_softplus(-(qkvf_ref[:, 3 * ATT_WIDTH:] + fb))
    c = cum_ref[...] + _cumsum_rows(log_f, _tril_ones(ts))
    cum_ref[...] = c[ts - 1:ts, :]
    for t, term in enumerate(_split3(c * (-LOG2E))):
        bias_ref[t] = term.astype(F32)


def _att_operands_pair(j, qkvf_ref, bias_ref, qg, kg, qt_ref, ka_ref, vt_ref):
    ts = qkvf_ref.shape[0]
    lane = lax.broadcasted_iota(jnp.int32, (ts, V7X_LANES), 1)
    lo_half = lane < ATT_HEAD_DIM
    q_scale = ATT_HEAD_DIM ** -0.5 * LOG2E

    def normed(x, g):
        sq = x * x
        ss_lo = jnp.sum(jnp.where(lo_half, sq, 0.0), axis=1, keepdims=True)
        ss_hi = jnp.sum(jnp.where(lo_half, 0.0, sq), axis=1, keepdims=True)
        inv = jnp.where(lo_half, lax.rsqrt(ss_lo / ATT_HEAD_DIM + EPS),
                        lax.rsqrt(ss_hi / ATT_HEAD_DIM + EPS))
        return x * inv * g

    lo, hi = V7X_LANES * j, V7X_LANES * (j + 1)
    qn = normed(qkvf_ref[:, lo:hi], qg) * q_scale
    kn = normed(qkvf_ref[:, ATT_WIDTH + lo:ATT_WIDTH + hi], kg)
    vp_t = qkvf_ref[:, 2 * ATT_WIDTH + lo:2 * ATT_WIDTH + hi].T
    pad_rows = ATT_VT_ROWS - ATT_HEAD_DIM
    ones_row = jnp.where(lax.broadcasted_iota(jnp.int32, (pad_rows, ts), 0) == 0, 1.0, 0.0)
    vt_ref[0, 2 * j, 0] = jnp.concatenate([vp_t[:ATT_HEAD_DIM, :], ones_row], axis=0).astype(BF16)
    vt_ref[0, 2 * j + 1, 0] = jnp.concatenate([ones_row, vp_t[ATT_HEAD_DIM:, :]], axis=0).astype(BF16)
    qn_t = qn.T
    ones3_rows = jnp.where(lax.broadcasted_iota(jnp.int32, (ATT_HEAD_DIM, ts), 0) < 3, 1.0, 0.0)
    qt_ref[0, 2 * j, 0] = jnp.concatenate([qn_t[:ATT_HEAD_DIM, :], ones3_rows], axis=0).astype(BF16)
    qt_ref[0, 2 * j + 1, 0] = jnp.concatenate([qn_t[ATT_HEAD_DIM:, :], ones3_rows], axis=0).astype(BF16)
    for parity in range(2):
        h = 2 * j + parity
        kh = kn if parity == 0 else pltpu.roll(kn, ATT_HEAD_DIM, axis=1)
        bias = jnp.where(lane == ATT_HEAD_DIM, bias_ref[0, :, h:h + 1],
                         jnp.where(lane == ATT_HEAD_DIM + 1, bias_ref[1, :, h:h + 1],
                                   jnp.where(lane == ATT_HEAD_DIM + 2, bias_ref[2, :, h:h + 1], 0.0)))
        ka_ref[0, h] = jnp.where(lo_half, kh, bias).astype(BF16)


def _inproj_kernel(x_ref, g_ref, w_ref, qg_ref, kg_ref, fb_ref, scw_ref, scb_ref, lcw_ref, lcb_ref,
                   qt_ref, ka_ref, vt_ref, zatt_ref, zssd_ref, xbc_ref, xlru_ref, zlru_ref, dt_ref,
                   cum_ref, spad_ref, lpad_ref, qkvf_ref, bias_ref):
    tm = x_ref.shape[1]

    @pl.when(pl.program_id(1) == 0)
    def _():
        cum_ref[...] = jnp.zeros_like(cum_ref)
        spad_ref[0:HALO, :] = jnp.zeros((HALO, SSD_CONV_DIM), F32)
        lpad_ref[0:HALO, :] = jnp.zeros((HALO, LRU_WIDTH), F32)

    x = x_ref[0]
    ms = jnp.mean(x * x, axis=-1, keepdims=True)
    u = (x * lax.rsqrt(ms + EPS) * g_ref[...]).astype(BF16)

    def proj(name, cols=None):
        lo, hi = _SEG[name]
        if cols is not None:
            lo, hi = lo + cols.start, lo + cols.stop
        return jnp.dot(u, w_ref[:, lo:hi], preferred_element_type=F32)

    chunk = lambda c: slice(c * INPROJ_CHUNK, (c + 1) * INPROJ_CHUNK)
    plain = [(ref, name, chunk(c))
             for ref, name, width in ((zssd_ref, "z_ssd", SSD_WIDTH), (zatt_ref, "z_att", ATT_WIDTH),
                                      (zlru_ref, "z_lru", LRU_WIDTH))
             for c in range(width // INPROJ_CHUNK)]
    convs = [(spad_ref, scw_ref, scb_ref, xbc_ref, "xbc", chunk(c), True)
             for c in range(SSD_CONV_DIM // INPROJ_CHUNK)]
    convs += [(lpad_ref, lcw_ref, lcb_ref, xlru_ref, "x_lru", chunk(c), False)
              for c in range(LRU_WIDTH // INPROJ_CHUNK)]

    def matmul_plain():
        if plain:
            ref, name, cols = plain.pop(0)
            ref[0, :, cols] = proj(name, cols)

    def matmul_conv(ci):
        if ci < len(convs):
            pad_ref, _, _, _, name, cols, _ = convs[ci]
            pad_ref[HALO:HALO + tm, cols] = proj(name, cols)

    def finish_conv(ci):
        pad_ref, cw_ref, cb_ref, out_ref, _, cols, silu = convs[ci]
        y = _causal_conv(pad_ref, cw_ref, cb_ref, cols)
        out_ref[0, :, cols] = _silu(y) if silu else y

    for si, name in enumerate(("q", "k", "v")):
        qkvf_ref[:, si * ATT_WIDTH:(si + 1) * ATT_WIDTH] = proj(name)
    fdt = proj("fdt")
    qkvf_ref[:, 3 * ATT_WIDTH:] = fdt
    dt_ref[0] = fdt
    matmul_conv(0)
    matmul_plain()
    _forget_bias_terms(qkvf_ref, fb_ref[...], cum_ref, bias_ref)
    pairs = list(range(ATT_HEADS // 2))
    for ci in range(len(convs)):
        matmul_conv(ci + 1)
        if pairs:
            _att_operands_pair(pairs.pop(0), qkvf_ref, bias_ref, qg_ref[...], kg_ref[...],
                               qt_ref, ka_ref, vt_ref)
        matmul_plain()
        finish_conv(ci)
    assert not pairs and not plain


def _inproj(h, g, w, qg, kg, fb, ssd_cw, ssd_cb, lru_cw, lru_cb):
    b, s, _ = h.shape
    tm = TM_INPROJ
    tok = lambda wd: pl.BlockSpec((1, tm, wd), lambda i, t: (i, t, 0))
    par = lambda r, wd: pl.BlockSpec((r, wd), lambda i, t: (0, 0))
    head = pl.BlockSpec((1, ATT_HEADS, tm, V7X_LANES), lambda i, t: (i, 0, t, 0))
    head_t = pl.BlockSpec((1, ATT_HEADS, 1, ATT_VT_ROWS, tm), lambda i, t: (i, 0, t, 0, 0))
    head_qt = pl.BlockSpec((1, ATT_HEADS, 1, V7X_LANES, tm), lambda i, t: (i, 0, t, 0, 0))
    tok_widths = (ATT_WIDTH, SSD_WIDTH, SSD_CONV_DIM, LRU_WIDTH, LRU_WIDTH, V7X_LANES)
    nbytes = 2 * (tm * D_MODEL * 4 + D_MODEL * D_IN_PAD * 2 + tm * sum(tok_widths) * 4
                  + 3 * ATT_HEADS * tm * V7X_LANES * 2) \
        + (HALO + tm) * (SSD_CONV_DIM + LRU_WIDTH) * 4 + 4 * tm * SSD_CONV_DIM * 4
    return pl.pallas_call(
        _inproj_kernel,
        grid=(b, s // tm),
        in_specs=[tok(D_MODEL), par(1, D_MODEL), par(D_MODEL, D_IN_PAD),
                  par(1, V7X_LANES), par(1, V7X_LANES), par(1, V7X_LANES),
                  par(CONV_WIDTH, SSD_CONV_DIM), par(1, SSD_CONV_DIM),
                  par(CONV_WIDTH, LRU_WIDTH), par(1, LRU_WIDTH)],
        out_specs=[head_qt, head, head_t] + [tok(wd) for wd in tok_widths],
        out_shape=[jax.ShapeDtypeStruct((b, ATT_HEADS, s // tm, V7X_LANES, tm), BF16),
                   jax.ShapeDtypeStruct((b, ATT_HEADS, s, V7X_LANES), BF16),
                   jax.ShapeDtypeStruct((b, ATT_HEADS, s // tm, ATT_VT_ROWS, tm), BF16)]
        + [jax.ShapeDtypeStruct((b, s, wd), F32) for wd in tok_widths],
        scratch_shapes=[pltpu.VMEM((1, V7X_LANES), F32),
                        pltpu.VMEM((HALO + tm, SSD_CONV_DIM), F32),
                        pltpu.VMEM((HALO + tm, LRU_WIDTH), F32),
                        pltpu.VMEM((tm, 3 * ATT_WIDTH + V7X_LANES), F32),
                        pltpu.VMEM((3, tm, V7X_LANES), F32)],
        compiler_params=_params(("parallel", "arbitrary"), nbytes),
        name="inproj",
    )(h, g, w, qg, kg, fb, ssd_cw, ssd_cb, lru_cw, lru_cb)


def _att_kernel(qt_ref, ka_ref, vt_ref, z_ref, o_ref, s_ref, cmax_ref, m_ref, acc_ref,
                *, tq, tk, tv, heads):
    qi = pl.program_id(2)
    subs = tq // tk
    tqb = qt_ref.shape[-1]
    assert tk % tqb == 0
    chains = [(sub, hh) for sub in range(subs) for hh in range(heads)]
    slots = ATT_LOOKAHEAD + 1

    m_ref[...] = jnp.full(m_ref.shape, -jnp.inf, F32)
    acc_ref[...] = jnp.zeros(acc_ref.shape, F32)

    assert len(chains) % slots == 0

    def causal_mask(s_t, sub, q0):
        key_r = lax.broadcasted_iota(jnp.int32, s_t.shape, 0)
        qry_c = lax.broadcasted_iota(jnp.int32, s_t.shape, 1)
        return jnp.where(key_r + (sub * tk - q0) <= qry_c, s_t, -jnp.inf)

    def issue_logits(j, idx, q0=0, diagonal=False):
        sub, hh = chains[idx]
        k0 = pl.multiple_of((j * subs + sub) * tk, tk)
        keys = ka_ref[0, hh, pl.ds(k0, tk), :]
        for qb in range(q0 // tqb, tq // tqb):
            cols = slice(qb * tqb, (qb + 1) * tqb)
            s_t = jnp.dot(keys, qt_ref[0, hh, qb], preferred_element_type=F32)
            if diagonal:
                s_t = causal_mask(s_t, sub, qb * tqb)
            s_ref[idx % slots, :, cols] = s_t
            cmax_ref[idx % slots, :, cols] = jnp.max(s_t, axis=0, keepdims=True)

    def step(j, diagonal):
        for idx, (sub, hh) in enumerate(chains):
            q0 = sub * tk if diagonal else 0
            s_t = s_ref[idx % slots, :, q0:]
            cmax = cmax_ref[idx % slots, :, q0:]
            ahead = idx + ATT_LOOKAHEAD
            if ahead < len(chains):
                issue_logits(j, ahead, chains[ahead][0] * tk if diagonal else 0, diagonal)
            elif not diagonal:
                issue_logits(j + 1, ahead - len(chains))
            if diagonal and idx < ATT_LOOKAHEAD:
                s_t = causal_mask(s_t, sub, q0)
                cmax = jnp.max(s_t, axis=0, keepdims=True)
            m = m_ref[hh, :, q0:]
            m_new = jnp.maximum(m, cmax)
            m_ref[hh, :, q0:] = m_new
            p_t = jnp.exp2(s_t - m_new).astype(BF16)
            acc = acc_ref[hh, :, q0:] * jnp.exp2(m - m_new)
            for vb in range(tk // tv):
                acc = acc + jnp.dot(vt_ref[0, hh, (j * subs + sub) * (tk // tv) + vb],
                                    p_t[vb * tv:(vb + 1) * tv, :], preferred_element_type=F32)
            acc_ref[hh, :, q0:] = acc

    for idx in range(ATT_LOOKAHEAD):
        issue_logits(0, idx)

    @pl.loop(0, qi)
    def _(j):
        step(j, False)

    step(qi, True)

    lead = ATT_VT_ROWS - ATT_HEAD_DIM
    for pr in range(heads // 2):
        even, odd = acc_ref[2 * pr], acc_ref[2 * pr + 1]
        o_t = jnp.concatenate(
            [even[0:ATT_HEAD_DIM, :] * (1.0 / even[ATT_HEAD_DIM:ATT_HEAD_DIM + 1, :]),
             odd[lead:, :] * (1.0 / odd[0:1, :])], axis=0)
        lanes = slice(V7X_LANES * pr, V7X_LANES * (pr + 1))
        o_ref[0, :, lanes] = (o_t.T * _silu(z_ref[0, :, lanes])).astype(o_ref.dtype)


def _attention(qt, ka, vt, z_att):
    b, _, s, _ = ka.shape
    tq, tk, hp = TQ_ATT, TK_ATT, ATT_HEADS_PER_STEP
    assert tq % tk == 0 and hp % 2 == 0 and ATT_HEADS % hp == 0
    wd = hp * ATT_HEAD_DIM
    tqb = qt.shape[-1]
    q_tile = pl.BlockSpec((1, hp, tq // tqb, V7X_LANES, tqb), lambda i, j, t: (i, j, t, 0, 0))
    k_full = pl.BlockSpec((1, hp, s, V7X_LANES), lambda i, j, t: (i, j, 0, 0))
    tv = vt.shape[-1]
    assert tk % tv == 0
    v_full = pl.BlockSpec((1, hp, s // tv, ATT_VT_ROWS, tv), lambda i, j, t: (i, j, 0, 0, 0))
    tok = pl.BlockSpec((1, tq, wd), lambda i, j, t: (i, t, j))
    nbytes = 2 * (hp * tq * V7X_LANES * 2 + 2 * hp * s * V7X_LANES * 2 + tq * wd * 6) \
        + (2 * ATT_LOOKAHEAD + 4) * tq * tk * 4
    return pl.pallas_call(
        functools.partial(_att_kernel, tq=tq, tk=tk, tv=tv, heads=hp),
        grid=(b, ATT_HEADS // hp, s // tq),
        in_specs=[q_tile, k_full, v_full, tok],
        out_specs=tok,
        out_shape=jax.ShapeDtypeStruct((b, s, ATT_WIDTH), BF16),
        scratch_shapes=[pltpu.VMEM((ATT_LOOKAHEAD + 1, tk, tq), F32),
                        pltpu.VMEM((ATT_LOOKAHEAD + 1, 1, tq), F32),
                        pltpu.VMEM((hp, 1, tq), F32),
                        pltpu.VMEM((hp, ATT_VT_ROWS, tq), F32)],
        compiler_params=_params(("parallel", "parallel", "arbitrary"), nbytes),
        name="att",
    )(qt, ka, vt, z_att)


def _ssd_kernel(xbc_ref, z_ref, dt_ref, dtb_ref, alog_ref, dsk_ref, ng_ref, y_ref, state_ref,
                *, chunks):
    L = SSD_CHUNK
    n = SSD_STATE
    gw = SSD_WIDTH // SSD_GROUPS

    @pl.when(pl.program_id(1) == 0)
    def _():
        state_ref[...] = jnp.zeros_like(state_ref)

    row = lax.broadcasted_iota(jnp.int32, (L, L), 0)
    col = lax.broadcasted_iota(jnp.int32, (L, L), 1)
    causal = col <= row
    tri = jnp.where(causal, 1.0, 0.0).astype(BF16)
    lo_half = lax.broadcasted_iota(jnp.int32, (L, V7X_LANES), 1) < SSD_HEAD_DIM
    a_neg = -jnp.exp(alog_ref[...])

    def decay_terms(ci):
        rows = slice(ci * L, (ci + 1) * L)
        dt = _softplus(dt_ref[0, rows, :] + dtb_ref[...])
        acs = _cumsum_rows(dt * a_neg, tri)
        tot = acs[L - 1:L, :]
        return (acs, jnp.exp(acs), jnp.exp(tot), acs.T, dt.T, (dt * jnp.exp(tot - acs)).T)

    terms = decay_terms(0)
    for ci in range(chunks):
        rows = slice(ci * L, (ci + 1) * L)
        acs, e_acs, e_tot, acs_t, dt_t, w_t = terms
        if ci + 1 < chunks:
            terms = decay_terms(ci + 1)

        y_tiles = []
        for g in range(SSD_GROUPS):
            b_lo = SSD_WIDTH + n * g
            c_lo = SSD_WIDTH + SSD_GROUPS * n + n * g
            bg_t = xbc_ref[0, rows, b_lo:b_lo + n].T
            cg = xbc_ref[0, rows, c_lo:c_lo + n].astype(BF16)
            cb = jnp.dot(cg, bg_t.astype(BF16), preferred_element_type=F32)
            s_prev = state_ref[g]
            cs = jnp.dot(cg, s_prev.astype(BF16), preferred_element_type=F32)
            for jj in range(gw // V7X_LANES):
                j = g * (gw // V7X_LANES) + jj
                he, ho = SSD_DT_LANE0 + 2 * j, SSD_DT_LANE0 + 2 * j + 1
                xp = xbc_ref[0, rows, V7X_LANES * j:V7X_LANES * (j + 1)]
                rhs = jnp.concatenate([jnp.where(lo_half, xp, 0.0), jnp.where(lo_half, 0.0, xp)],
                                      axis=0).astype(BF16)

                def scores(h):
                    seg = acs[:, h:h + 1] - acs_t[h:h + 1, :]
                    return cb * jnp.exp(jnp.where(causal, seg, -jnp.inf)) * dt_t[h:h + 1, :]

                lhs = jnp.concatenate([scores(he), scores(ho)], axis=1).astype(BF16)
                y_diag = jnp.dot(lhs, rhs, preferred_element_type=F32)
                e_pair = jnp.where(lo_half, e_acs[:, he:he + 1], e_acs[:, ho:ho + 1])
                y_off = cs[:, V7X_LANES * jj:V7X_LANES * (jj + 1)] * e_pair
                y_tiles.append(y_diag + y_off + xp * dsk_ref[:, V7X_LANES * j:V7X_LANES * (j + 1)])

                lhs_s = jnp.concatenate([bg_t * w_t[he:he + 1, :], bg_t * w_t[ho:ho + 1, :]],
                                        axis=1).astype(BF16)
                s_new = jnp.dot(lhs_s, rhs, preferred_element_type=F32)
                dec = jnp.where(lo_half[0:1], e_tot[:, he:he + 1], e_tot[:, ho:ho + 1])
                state_ref[g, :, V7X_LANES * jj:V7X_LANES * (jj + 1)] = (
                    s_prev[:, V7X_LANES * jj:V7X_LANES * (jj + 1)] * dec + s_new)

        y = jnp.concatenate(y_tiles, axis=1)
        gated = y * _silu(z_ref[0, rows, :])
        outs = []
        for g in range(SSD_GROUPS):
            gg = gated[:, gw * g:gw * (g + 1)]
            ms = jnp.mean(gg * gg, axis=-1, keepdims=True)
            outs.append(gg * lax.rsqrt(ms + EPS))
        y_ref[0, rows, :] = (jnp.concatenate(outs, axis=1) * ng_ref[...]).astype(y_ref.dtype)


def _ssd(xbc, z, dt_raw, dt_bias, a_log, d_skip, norm_g):
    b, s, _ = xbc.shape
    chunks = SSD_CHUNKS_PER_STEP
    ts = chunks * SSD_CHUNK
    tok = lambda wd: pl.BlockSpec((1, ts, wd), lambda i, t: (i, t, 0))
    par = lambda r, wd: pl.BlockSpec((r, wd), lambda i, t: (0, 0))
    nbytes = 2 * ts * (SSD_CONV_DIM + SSD_WIDTH + V7X_LANES) * 4 + 2 * ts * SSD_WIDTH * 2 \
        + SSD_GROUPS * SSD_STATE * SSD_WIDTH * 2 + 8 * SSD_CHUNK * SSD_CONV_DIM * 4
    return pl.pallas_call(
        functools.partial(_ssd_kernel, chunks=chunks),
        grid=(b, s // ts),
        in_specs=[tok(SSD_CONV_DIM), tok(SSD_WIDTH), tok(V7X_LANES),
                  par(1, V7X_LANES), par(1, V7X_LANES), par(1, SSD_WIDTH), par(1, SSD_WIDTH)],
        out_specs=tok(SSD_WIDTH),
        out_shape=jax.ShapeDtypeStruct((b, s, SSD_WIDTH), BF16),
        scratch_shapes=[pltpu.VMEM((SSD_GROUPS, SSD_STATE, SSD_WIDTH // SSD_GROUPS), F32)],
        compiler_params=_params(("parallel", "arbitrary"), nbytes),
        name="ssd",
    )(xbc, z, dt_raw, dt_bias, a_log, d_skip, norm_g)


def _lru_kernel(x_ref, z_ref, wg_ref, bg_ref, lam_ref, y_ref, a_ref, b_ref, h_ref):
    ts = x_ref.shape[1]
    groups = ts // V7X_SUBLANES
    first_tile = pl.program_id(1) == 0

    @pl.when(first_tile)
    def _():
        h_ref[...] = jnp.zeros_like(h_ref)

    xc = x_ref[0]
    gates = jnp.dot(xc.astype(BF16), wg_ref[...], preferred_element_type=F32) + bg_ref[...]
    r = _sigmoid(gates[:, :LRU_WIDTH])
    i = _sigmoid(gates[:, LRU_WIDTH:])
    a = jnp.exp2(r * ((-LRU_C * LOG2E) * _softplus(-lam_ref[...])))
    gap = 1.0 - a * a
    mult = jnp.where(gap > 0.0, gap * lax.rsqrt(gap), 0.0)
    seq_start = first_tile & (lax.broadcasted_iota(jnp.int32, (ts, LRU_WIDTH), 0) == 0)
    mult = jnp.where(seq_start, 1.0, mult)
    b = mult * (i * xc)

    aa = a.reshape(groups, V7X_SUBLANES, LRU_WIDTH)
    bb = b.reshape(groups, V7X_SUBLANES, LRU_WIDTH)
    sub = lax.broadcasted_iota(jnp.int32, (groups, V7X_SUBLANES, LRU_WIDTH), 1)
    for d in (1, 2, 4):
        keep = sub >= d
        bb = jnp.where(keep, aa * pltpu.roll(bb, d, axis=1) + bb, bb)
        aa = jnp.where(keep, aa * pltpu.roll(aa, d, axis=1), aa)
    a_ref[...] = aa.reshape(ts, LRU_WIDTH)
    b_ref[...] = bb.reshape(ts, LRU_WIDTH)

    def group(gi, h):
        r0 = pl.multiple_of(gi * V7X_SUBLANES, V7X_SUBLANES)
        rows = pl.ds(r0, V7X_SUBLANES)
        hh = a_ref[rows, :] * h + b_ref[rows, :]
        b_ref[rows, :] = hh
        return hh[V7X_SUBLANES - 1:V7X_SUBLANES, :]

    h_ref[...] = lax.fori_loop(0, groups, group, h_ref[...], unroll=LRU_UNROLL)
    y_ref[0] = (b_ref[...] * _silu(z_ref[0])).astype(y_ref.dtype)


def _lru(x_lru, z_lru, w_gate, b_gate, lam):
    b, s, _ = x_lru.shape
    ts = TS_LRU
    tok = pl.BlockSpec((1, ts, LRU_WIDTH), lambda i, t: (i, t, 0))
    par = lambda r, wd: pl.BlockSpec((r, wd), lambda i, t: (0, 0))
    nbytes = 2 * 2 * ts * LRU_WIDTH * 4 + 2 * ts * LRU_WIDTH * 2 + 2 * LRU_WIDTH * 2 * LRU_WIDTH * 2 \
        + 2 * ts * LRU_WIDTH * 4 + 8 * ts * LRU_WIDTH * 4
    return pl.pallas_call(
        _lru_kernel,
        grid=(b, s // ts),
        in_specs=[tok, tok, par(LRU_WIDTH, 2 * LRU_WIDTH), par(1, 2 * LRU_WIDTH), par(1, LRU_WIDTH)],
        out_specs=tok,
        out_shape=jax.ShapeDtypeStruct((b, s, LRU_WIDTH), BF16),
        scratch_shapes=[pltpu.VMEM((ts, LRU_WIDTH), F32),
                        pltpu.VMEM((ts, LRU_WIDTH), F32),
                        pltpu.VMEM((1, LRU_WIDTH), F32)],
        compiler_params=_params(("parallel", "arbitrary"), nbytes),
        name="lru",
    )(x_lru, z_lru, w_gate, b_gate, lam)


def _outproj_kernel(h_ref, ya_ref, ys_ref, yl_ref, w_ref, o_ref):
    a0, a1, a2 = ATT_WIDTH, ATT_WIDTH + SSD_WIDTH, MIX_WIDTH
    o_ref[...] = (h_ref[...]
                  + jnp.dot(ya_ref[...], w_ref[0:a0, :], preferred_element_type=F32)
                  + jnp.dot(ys_ref[...], w_ref[a0:a1, :], preferred_element_type=F32)
                  + jnp.dot(yl_ref[...], w_ref[a1:a2, :], preferred_element_type=F32))


def _outproj(h, y_att, y_ssd, y_lru, w):
    t = h.shape[0]
    tm = TM_OUTPROJ
    tok = lambda wd: pl.BlockSpec((tm, wd), lambda i: (i, 0))
    nbytes = 2 * (2 * tm * D_MODEL * 4 + tm * MIX_WIDTH * 2 + MIX_WIDTH * D_MODEL * 2)
    return pl.pallas_call(
        _outproj_kernel,
        grid=(t // tm,),
        in_specs=[tok(D_MODEL), tok(ATT_WIDTH), tok(SSD_WIDTH), tok(LRU_WIDTH),
                  pl.BlockSpec((MIX_WIDTH, D_MODEL), lambda i: (0, 0))],
        out_specs=tok(D_MODEL),
        out_shape=jax.ShapeDtypeStruct((t, D_MODEL), F32),
        compiler_params=_params(("parallel",), nbytes),
        name="outproj",
    )(h, y_att, y_ssd, y_lru, w)


def _pad_lanes(v, lane0=0, width=V7X_LANES):
    return jnp.pad(v, (lane0, width - lane0 - v.shape[0]))[None, :]


def _regroup_w_in(w):
    parts = dict(zip(IN_NAMES, jnp.split(w, [sum(IN_SIZES[:i]) for i in range(1, len(IN_SIZES))],
                                         axis=1)))
    parts["fdt"] = jnp.concatenate([parts["f_raw"], parts["dt_raw"]], axis=1)
    cols = [jnp.pad(parts[name], ((0, 0), (0, width - parts[name].shape[1]))) for name, width in _SEGS]
    return jnp.concatenate(cols, axis=1).astype(BF16)


def _block_diag(w):
    nb, d, e = w.shape
    eye = jnp.eye(nb, dtype=w.dtype)
    return jnp.einsum("nde,nm->ndme", w, eye).reshape(nb * d, nb * e)


def kernel(x, norm_g, w_in, q_norm_g, k_norm_g, forget_b, ssd_conv_w, ssd_conv_b, ssd_dt_bias,
           ssd_a_log, ssd_d, ssd_norm_g, lru_conv_w, lru_conv_b, lru_w_a, lru_b_a, lru_w_x, lru_b_x,
           lru_lambda, w_out):
    bsz, seq, d = x.shape
    assert d == D_MODEL and seq % max(TQ_ATT, TM_INPROJ, TS_LRU, SSD_CHUNKS_PER_STEP * SSD_CHUNK) == 0
    assert (bsz * seq) % TM_OUTPROJ == 0
    depth = w_in.shape[0]
    h = x.astype(F32).reshape(bsz * seq, D_MODEL)
    for l in range(depth):
        qt, ka, vt, z_att, z_ssd, xbc, x_lru, z_lru, dt_raw = _inproj(
            h.reshape(bsz, seq, D_MODEL), norm_g[l][None, :], _regroup_w_in(w_in[l]),
            jnp.tile(q_norm_g[l], 2)[None, :], jnp.tile(k_norm_g[l], 2)[None, :],
            _pad_lanes(forget_b[l]), ssd_conv_w[l], ssd_conv_b[l][None, :],
            lru_conv_w[l], lru_conv_b[l][None, :])
        y_att = _attention(qt, ka, vt, z_att)
        y_ssd = _ssd(xbc, z_ssd, dt_raw, _pad_lanes(ssd_dt_bias[l], SSD_DT_LANE0),
                     _pad_lanes(ssd_a_log[l], SSD_DT_LANE0),
                     jnp.repeat(ssd_d[l], SSD_HEAD_DIM)[None, :], ssd_norm_g[l][None, :])
        w_gate = jnp.concatenate([_block_diag(lru_w_a[l]), _block_diag(lru_w_x[l])], axis=1).astype(BF16)
        b_gate = jnp.concatenate([lru_b_a[l], lru_b_x[l]])[None, :]
        y_lru = _lru(x_lru, z_lru, w_gate, b_gate, lru_lambda[l][None, :])
        h = _outproj(h, y_att.reshape(bsz * seq, ATT_WIDTH), y_ssd.reshape(bsz * seq, SSD_WIDTH),
                     y_lru.reshape(bsz * seq, LRU_WIDTH), w_out[l].astype(BF16))
    return h.reshape(bsz, seq, D_MODEL).astype(x.dtype)
```

```python
import functools

import jax
import jax.numpy as jnp
from jax import lax
from jax.experimental import pallas as pl
from jax.experimental.pallas import tpu as pltpu

F32 = jnp.float32
BF16 = jnp.bfloat16

V7X_LANES = 128
V7X_SUBLANES = 8
V7X_VMEM_BYTES = 64 * 1024 * 1024

D_MODEL = 1024
EPS = 1e-6
CONV_WIDTH = 4
MIX_WIDTH = 2 * D_MODEL
ATT_HEAD_DIM = 64
ATT_WIDTH = MIX_WIDTH // 4
ATT_HEADS = ATT_WIDTH // ATT_HEAD_DIM
SSD_HEAD_DIM = 64
SSD_WIDTH = MIX_WIDTH // 2
SSD_HEADS = SSD_WIDTH // SSD_HEAD_DIM
SSD_GROUPS = 2
SSD_STATE = 128
SSD_CHUNK = 128
SSD_CONV_DIM = SSD_WIDTH + 2 * SSD_GROUPS * SSD_STATE
LRU_WIDTH = MIX_WIDTH // 4
LRU_BLOCKS = 8
LRU_C = 8.0
IN_SIZES = (ATT_WIDTH, ATT_WIDTH, ATT_WIDTH, ATT_WIDTH, ATT_HEADS,
            SSD_WIDTH, SSD_CONV_DIM, SSD_HEADS, LRU_WIDTH, LRU_WIDTH)
IN_NAMES = ("q", "k", "v", "z_att", "f_raw", "z_ssd", "xbc", "dt_raw", "x_lru", "z_lru")

LOG2E = 1.4426950408889634
HALO = V7X_SUBLANES

SSD_DT_LANE0 = ATT_HEADS
_SEGS = (("q", ATT_WIDTH), ("k", ATT_WIDTH), ("v", ATT_WIDTH), ("z_att", ATT_WIDTH),
         ("z_ssd", SSD_WIDTH), ("xbc", SSD_CONV_DIM), ("x_lru", LRU_WIDTH), ("z_lru", LRU_WIDTH),
         ("fdt", V7X_LANES))
_SEG = {}
for _name, _width in _SEGS:
    _SEG[_name] = (sum(w for _, w in _SEGS[:len(_SEG)]), sum(w for _, w in _SEGS[:len(_SEG)]) + _width)
D_IN_PAD = sum(w for _, w in _SEGS)

TQ_ATT = 512
TK_ATT = 256
ATT_VT_ROWS = 80
ATT_HEADS_PER_STEP = 8
ATT_LOOKAHEAD = 3
TM_INPROJ = 256
INPROJ_CHUNK = 256
SSD_CHUNKS_PER_STEP = 2
TS_LRU = 1024
LRU_UNROLL = 4
TM_OUTPROJ = 1024


def _vmem_limit(nbytes):
    return int(min(nbytes * 3 // 2 + (8 << 20), V7X_VMEM_BYTES - (6 << 20)))


def _params(semantics, nbytes):
    return pltpu.CompilerParams(dimension_semantics=semantics, vmem_limit_bytes=_vmem_limit(nbytes))


def _sigmoid(x):
    return 1.0 / (1.0 + jnp.exp(-x))


def _silu(x):
    return x * _sigmoid(x)


def _softplus(x):
    return jnp.maximum(x, 0.0) + jnp.log(1.0 + jnp.exp(-jnp.abs(x)))


def _split3(x):
    hi = x.astype(BF16)
    r1 = x - hi.astype(F32)
    mid = r1.astype(BF16)
    lo = (r1 - mid.astype(F32)).astype(BF16)
    return hi, mid, lo


def _tril_ones(n):
    row = lax.broadcasted_iota(jnp.int32, (n, n), 0)
    col = lax.broadcasted_iota(jnp.int32, (n, n), 1)
    return jnp.where(col <= row, 1.0, 0.0).astype(BF16)


def _cumsum_rows(x, tri):
    hi, mid, lo = _split3(x)
    dot = lambda a: jnp.dot(tri, a, preferred_element_type=F32)
    return dot(hi) + dot(mid) + dot(lo)


def _causal_conv(pad_ref, w_ref, b_ref, cols):
    rows = pad_ref.shape[0] - HALO
    xp = pad_ref[:, cols]
    x = xp[HALO:, :]
    y = b_ref[:, cols] + w_ref[CONV_WIDTH - 1:CONV_WIDTH, cols] * x
    for shift in range(1, CONV_WIDTH):
        kk = CONV_WIDTH - 1 - shift
        y = y + w_ref[kk:kk + 1, cols] * pltpu.roll(xp, shift, axis=0)[HALO:, :]
    pad_ref[0:HALO, cols] = x[rows - HALO:rows, :]
    return y


def _forget_bias_terms(qkvf_ref, fb, cum_ref, bias_ref):
    ts = qkvf_ref.shape[0]
    log_f = -_softplus(-(qkvf_ref[:, 3 * ATT_WIDTH:] + fb))
    c = cum_ref[...] + _cumsum_rows(log_f, _tril_ones(ts))
    cum_ref[...] = c[ts - 1:ts, :]
    for t, term in enumerate(_split3(c * (-LOG2E))):
        bias_ref[t] = term.astype(F32)


def _att_operands_pair(j, qkvf_ref, bias_ref, qg, kg, qt_ref, ka_ref, vt_ref):
    ts = qkvf_ref.shape[0]
    lane = lax.broadcasted_iota(jnp.int32, (ts, V7X_LANES), 1)
    lo_half = lane < ATT_HEAD_DIM
    q_scale = ATT_HEAD_DIM ** -0.5 * LOG2E

    def normed(x, g):
        sq = x * x
        ss_lo = jnp.sum(jnp.where(lo_half, sq, 0.0), axis=1, keepdims=True)
        ss_hi = jnp.sum(jnp.where(lo_half, 0.0, sq), axis=1, keepdims=True)
        inv = jnp.where(lo_half, lax.rsqrt(ss_lo / ATT_HEAD_DIM + EPS),
                        lax.rsqrt(ss_hi / ATT_HEAD_DIM + EPS))
        return x * inv * g

    lo, hi = V7X_LANES * j, V7X_LANES * (j + 1)
    qn = normed(qkvf_ref[:, lo:hi], qg) * q_scale
    kn = normed(qkvf_ref[:, ATT_WIDTH + lo:ATT_WIDTH + hi], kg)
    vp_t = qkvf_ref[:, 2 * ATT_WIDTH + lo:2 * ATT_WIDTH + hi].T
    pad_rows = ATT_VT_ROWS - ATT_HEAD_DIM
    ones_row = jnp.where(lax.broadcasted_iota(jnp.int32, (pad_rows, ts), 0) == 0, 1.0, 0.0)
    vt_ref[0, 2 * j, 0] = jnp.concatenate([vp_t[:ATT_HEAD_DIM, :], ones_row], axis=0).astype(BF16)
    vt_ref[0, 2 * j + 1, 0] = jnp.concatenate([ones_row, vp_t[ATT_HEAD_DIM:, :]], axis=0).astype(BF16)
    qn_t = qn.T
    ones3_rows = jnp.where(lax.broadcasted_iota(jnp.int32, (ATT_HEAD_DIM, ts), 0) < 3, 1.0, 0.0)
    qt_ref[0, 2 * j, 0] = jnp.concatenate([qn_t[:ATT_HEAD_DIM, :], ones3_rows], axis=0).astype(BF16)
    qt_ref[0, 2 * j + 1, 0] = jnp.concatenate([qn_t[ATT_HEAD_DIM:, :], ones3_rows], axis=0).astype(BF16)
    for parity in range(2):
        h = 2 * j + parity
        kh = kn if parity == 0 else pltpu.roll(kn, ATT_HEAD_DIM, axis=1)
        bias = jnp.where(lane == ATT_HEAD_DIM, bias_ref[0, :, h:h + 1],
                         jnp.where(lane == ATT_HEAD_DIM + 1, bias_ref[1, :, h:h + 1],
                                   jnp.where(lane == ATT_HEAD_DIM + 2, bias_ref[2, :, h:h + 1], 0.0)))
        ka_ref[0, h] = jnp.where(lo_half, kh, bias).astype(BF16)


def _inproj_kernel(x_ref, g_ref, w_ref, qg_ref, kg_ref, fb_ref, scw_ref, scb_ref, lcw_ref, lcb_ref,
                   qt_ref, ka_ref, vt_ref, zatt_ref, zssd_ref, xbc_ref, xlru_ref, zlru_ref, dt_ref,
                   cum_ref, spad_ref, lpad_ref, qkvf_ref, bias_ref):
    tm = x_ref.shape[1]

    @pl.when(pl.program_id(1) == 0)
    def _():
        cum_ref[...] = jnp.zeros_like(cum_ref)
        spad_ref[0:HALO, :] = jnp.zeros((HALO, SSD_CONV_DIM), F32)
        lpad_ref[0:HALO, :] = jnp.zeros((HALO, LRU_WIDTH), F32)

    x = x_ref[0]
    ms = jnp.mean(x * x, axis=-1, keepdims=True)
    u = (x * lax.rsqrt(ms + EPS) * g_ref[...]).astype(BF16)

    def proj(name, cols=None):
        lo, hi = _SEG[name]
        if cols is not None:
            lo, hi = lo + cols.start, lo + cols.stop
        return jnp.dot(u, w_ref[:, lo:hi], preferred_element_type=F32)

    chunk = lambda c: slice(c * INPROJ_CHUNK, (c + 1) * INPROJ_CHUNK)
    plain = [(ref, name, chunk(c))
             for ref, name, width in ((zssd_ref, "z_ssd", SSD_WIDTH), (zatt_ref, "z_att", ATT_WIDTH),
                                      (zlru_ref, "z_lru", LRU_WIDTH))
             for c in range(width // INPROJ_CHUNK)]
    convs = [(spad_ref, scw_ref, scb_ref, xbc_ref, "xbc", chunk(c), True)
             for c in range(SSD_CONV_DIM // INPROJ_CHUNK)]
    convs += [(lpad_ref, lcw_ref, lcb_ref, xlru_ref, "x_lru", chunk(c), False)
              for c in range(LRU_WIDTH // INPROJ_CHUNK)]

    def matmul_plain():
        if plain:
            ref, name, cols = plain.pop(0)
            ref[0, :, cols] = proj(name, cols)

    def matmul_conv(ci):
        if ci < len(convs):
            pad_ref, _, _, _, name, cols, _ = convs[ci]
            pad_ref[HALO:HALO + tm, cols] = proj(name, cols)

    def finish_conv(ci):
        pad_ref, cw_ref, cb_ref, out_ref, _, cols, silu = convs[ci]
        y = _causal_conv(pad_ref, cw_ref, cb_ref, cols)
        out_ref[0, :, cols] = _silu(y) if silu else y

    for si, name in enumerate(("q", "k", "v")):
        qkvf_ref[:, si * ATT_WIDTH:(si + 1) * ATT_WIDTH] = proj(name)
    fdt = proj("fdt")
    qkvf_ref[:, 3 * ATT_WIDTH:] = fdt
    dt_ref[0] = fdt
    matmul_conv(0)
    matmul_plain()
    _forget_bias_terms(qkvf_ref, fb_ref[...], cum_ref, bias_ref)
    pairs = list(range(ATT_HEADS // 2))
    for ci in range(len(convs)):
        matmul_conv(ci + 1)
        if pairs:
            _att_operands_pair(pairs.pop(0), qkvf_ref, bias_ref, qg_ref[...], kg_ref[...],
                               qt_ref, ka_ref, vt_ref)
        matmul_plain()
        finish_conv(ci)
    assert not pairs and not plain


def _inproj(h, g, w, qg, kg, fb, ssd_cw, ssd_cb, lru_cw, lru_cb):
    b, s, _ = h.shape
    tm = TM_INPROJ
    tok = lambda wd: pl.BlockSpec((1, tm, wd), lambda i, t: (i, t, 0))
    par = lambda r, wd: pl.BlockSpec((r, wd), lambda i, t: (0, 0))
    head = pl.BlockSpec((1, ATT_HEADS, tm, V7X_LANES), lambda i, t: (i, 0, t, 0))
    head_t = pl.BlockSpec((1, ATT_HEADS, 1, ATT_VT_ROWS, tm), lambda i, t: (i, 0, t, 0, 0))
    head_qt = pl.BlockSpec((1, ATT_HEADS, 1, V7X_LANES, tm), lambda i, t: (i, 0, t, 0, 0))
    tok_widths = (ATT_WIDTH, SSD_WIDTH, SSD_CONV_DIM, LRU_WIDTH, LRU_WIDTH, V7X_LANES)
    nbytes = 2 * (tm * D_MODEL * 4 + D_MODEL * D_IN_PAD * 2 + tm * sum(tok_widths) * 4
                  + 3 * ATT_HEADS * tm * V7X_LANES * 2) \
        + (HALO + tm) * (SSD_CONV_DIM + LRU_WIDTH) * 4 + 4 * tm * SSD_CONV_DIM * 4
    return pl.pallas_call(
        _inproj_kernel,
        grid=(b, s // tm),
        in_specs=[tok(D_MODEL), par(1, D_MODEL), par(D_MODEL, D_IN_PAD),
                  par(1, V7X_LANES), par(1, V7X_LANES), par(1, V7X_LANES),
                  par(CONV_WIDTH, SSD_CONV_DIM), par(1, SSD_CONV_DIM),
                  par(CONV_WIDTH, LRU_WIDTH), par(1, LRU_WIDTH)],
        out_specs=[head_qt, head, head_t] + [tok(wd) for wd in tok_widths],
        out_shape=[jax.ShapeDtypeStruct((b, ATT_HEADS, s // tm, V7X_LANES, tm), BF16),
                   jax.ShapeDtypeStruct((b, ATT_HEADS, s, V7X_LANES), BF16),
                   jax.ShapeDtypeStruct((b, ATT_HEADS, s // tm, ATT_VT_ROWS, tm), BF16)]
        + [jax.ShapeDtypeStruct((b, s, wd), F32) for wd in tok_widths],
        scratch_shapes=[pltpu.VMEM((1, V7X_LANES), F32),
                        pltpu.VMEM((HALO + tm, SSD_CONV_DIM), F32),
                        pltpu.VMEM((HALO + tm, LRU_WIDTH), F32),
                        pltpu.VMEM((tm, 3 * ATT_WIDTH + V7X_LANES), F32),
                        pltpu.VMEM((3, tm, V7X_LANES), F32)],
        compiler_params=_params(("parallel", "arbitrary"), nbytes),
        name="inproj",
    )(h, g, w, qg, kg, fb, ssd_cw, ssd_cb, lru_cw, lru_cb)


def _att_kernel(qt_ref, ka_ref, vt_ref, z_ref, o_ref, s_ref, cmax_ref, m_ref, acc_ref,
                *, tq, tk, tv, heads):
    qi = pl.program_id(2)
    subs = tq // tk
    tqb = qt_ref.shape[-1]
    assert tk % tqb == 0
    chains = [(sub, hh) for sub in range(subs) for hh in range(heads)]
    slots = ATT_LOOKAHEAD + 1

    m_ref[...] = jnp.full(m_ref.shape, -jnp.inf, F32)
    acc_ref[...] = jnp.zeros(acc_ref.shape, F32)

    assert len(chains) % slots == 0

    def causal_mask(s_t, sub, q0):
        key_r = lax.broadcasted_iota(jnp.int32, s_t.shape, 0)
        qry_c = lax.broadcasted_iota(jnp.int32, s_t.shape, 1)
        return jnp.where(key_r + (sub * tk - q0) <= qry_c, s_t, -jnp.inf)

    def issue_logits(j, idx, q0=0, diagonal=False):
        sub, hh = chains[idx]
        k0 = pl.multiple_of((j * subs + sub) * tk, tk)
        keys = ka_ref[0, hh, pl.ds(k0, tk), :]
        for qb in range(q0 // tqb, tq // tqb):
            cols = slice(qb * tqb, (qb + 1) * tqb)
            s_t = jnp.dot(keys, qt_ref[0, hh, qb], preferred_element_type=F32)
            if diagonal:
                s_t = causal_mask(s_t, sub, qb * tqb)
            s_ref[idx % slots, :, cols] = s_t
            cmax_ref[idx % slots, :, cols] = jnp.max(s_t, axis=0, keepdims=True)

    def step(j, diagonal):
        for idx, (sub, hh) in enumerate(chains):
            q0 = sub * tk if diagonal else 0
            s_t = s_ref[idx % slots, :, q0:]
            cmax = cmax_ref[idx % slots, :, q0:]
            ahead = idx + ATT_LOOKAHEAD
            if ahead < len(chains):
                issue_logits(j, ahead, chains[ahead][0] * tk if diagonal else 0, diagonal)
            elif not diagonal:
                issue_logits(j + 1, ahead - len(chains))
            if diagonal and idx < ATT_LOOKAHEAD:
                s_t = causal_mask(s_t, sub, q0)
                cmax = jnp.max(s_t, axis=0, keepdims=True)
            m = m_ref[hh, :, q0:]
            m_new = jnp.maximum(m, cmax)
            m_ref[hh, :, q0:] = m_new
            p_t = jnp.exp2(s_t - m_new).astype(BF16)
            acc = acc_ref[hh, :, q0:] * jnp.exp2(m - m_new)
            for vb in range(tk // tv):
                acc = acc + jnp.dot(vt_ref[0, hh, (j * subs + sub) * (tk // tv) + vb],
                                    p_t[vb * tv:(vb + 1) * tv, :], preferred_element_type=F32)
            acc_ref[hh, :, q0:] = acc

    for idx in range(ATT_LOOKAHEAD):
        issue_logits(0, idx)

    @pl.loop(0, qi)
    def _(j):
        step(j, False)

    step(qi, True)

    lead = ATT_VT_ROWS - ATT_HEAD_DIM
    for pr in range(heads // 2):
        even, odd = acc_ref[2 * pr], acc_ref[2 * pr + 1]
        o_t = jnp.concatenate(
            [even[0:ATT_HEAD_DIM, :] * (1.0 / even[ATT_HEAD_DIM:ATT_HEAD_DIM + 1, :]),
             odd[lead:, :] * (1.0 / odd[0:1, :])], axis=0)
        lanes = slice(V7X_LANES * pr, V7X_LANES * (pr + 1))
        o_ref[0, :, lanes] = (o_t.T * _silu(z_ref[0, :, lanes])).astype(o_ref.dtype)


def _attention(qt, ka, vt, z_att):
    b, _, s, _ = ka.shape
    tq, tk, hp = TQ_ATT, TK_ATT, ATT_HEADS_PER_STEP
    assert tq % tk == 0 and hp % 2 == 0 and ATT_HEADS % hp == 0
    wd = hp * ATT_HEAD_DIM
    tqb = qt.shape[-1]
    q_tile = pl.BlockSpec((1, hp, tq // tqb, V7X_LANES, tqb), lambda i, j, t: (i, j, t, 0, 0))
    k_full = pl.BlockSpec((1, hp, s, V7X_LANES), lambda i, j, t: (i, j, 0, 0))
    tv = vt.shape[-1]
    assert tk % tv == 0
    v_full = pl.BlockSpec((1, hp, s // tv, ATT_VT_ROWS, tv), lambda i, j, t: (i, j, 0, 0, 0))
    tok = pl.BlockSpec((1, tq, wd), lambda i, j, t: (i, t, j))
    nbytes = 2 * (hp * tq * V7X_LANES * 2 + 2 * hp * s * V7X_LANES * 2 + tq * wd * 6) \
        + (2 * ATT_LOOKAHEAD + 4) * tq * tk * 4
    return pl.pallas_call(
        functools.partial(_att_kernel, tq=tq, tk=tk, tv=tv, heads=hp),
        grid=(b, ATT_HEADS // hp, s // tq),
        in_specs=[q_tile, k_full, v_full, tok],
        out_specs=tok,
        out_shape=jax.ShapeDtypeStruct((b, s, ATT_WIDTH), BF16),
        scratch_shapes=[pltpu.VMEM((ATT_LOOKAHEAD + 1, tk, tq), F32),
                        pltpu.VMEM((ATT_LOOKAHEAD + 1, 1, tq), F32),
                        pltpu.VMEM((hp, 1, tq), F32),
                        pltpu.VMEM((hp, ATT_VT_ROWS, tq), F32)],
        compiler_params=_params(("parallel", "parallel", "arbitrary"), nbytes),
        name="att",
    )(qt, ka, vt, z_att)


def _ssd_kernel(xbc_ref, z_ref, dt_ref, dtb_ref, alog_ref, dsk_ref, ng_ref, y_ref, state_ref,
                *, chunks):
    L = SSD_CHUNK
    n = SSD_STATE
    gw = SSD_WIDTH // SSD_GROUPS

    @pl.when(pl.program_id(1) == 0)
    def _():
        state_ref[...] = jnp.zeros_like(state_ref)

    row = lax.broadcasted_iota(jnp.int32, (L, L), 0)
    col = lax.broadcasted_iota(jnp.int32, (L, L), 1)
    causal = col <= row
    tri = jnp.where(causal, 1.0, 0.0).astype(BF16)
    lo_half = lax.broadcasted_iota(jnp.int32, (L, V7X_LANES), 1) < SSD_HEAD_DIM
    a_neg = -jnp.exp(alog_ref[...])

    def decay_terms(ci):
        rows = slice(ci * L, (ci + 1) * L)
        dt = _softplus(dt_ref[0, rows, :] + dtb_ref[...])
        acs = _cumsum_rows(dt * a_neg, tri)
        tot = acs[L - 1:L, :]
        return (acs, jnp.exp(acs), jnp.exp(tot), acs.T, dt.T, (dt * jnp.exp(tot - acs)).T)

    terms = decay_terms(0)
    for ci in range(chunks):
        rows = slice(ci * L, (ci + 1) * L)
        acs, e_acs, e_tot, acs_t, dt_t, w_t = terms
        if ci + 1 < chunks:
            terms = decay_terms(ci + 1)

        y_tiles = []
        for g in range(SSD_GROUPS):
            b_lo = SSD_WIDTH + n * g
            c_lo = SSD_WIDTH + SSD_GROUPS * n + n * g
            bg_t = xbc_ref[0, rows, b_lo:b_lo + n].T
            cg = xbc_ref[0, rows, c_lo:c_lo + n].astype(BF16)
            cb = jnp.dot(cg, bg_t.astype(BF16), preferred_element_type=F32)
            s_prev = state_ref[g]
            cs = jnp.dot(cg, s_prev.astype(BF16), preferred_element_type=F32)
            for jj in range(gw // V7X_LANES):
                j = g * (gw // V7X_LANES) + jj
                he, ho = SSD_DT_LANE0 + 2 * j, SSD_DT_LANE0 + 2 * j + 1
                xp = xbc_ref[0, rows, V7X_LANES * j:V7X_LANES * (j + 1)]
                rhs = jnp.concatenate([jnp.where(lo_half, xp, 0.0), jnp.where(lo_half, 0.0, xp)],
                                      axis=0).astype(BF16)

                def scores(h):
                    seg = acs[:, h:h + 1] - acs_t[h:h + 1, :]
                    return cb * jnp.exp(jnp.where(causal, seg, -jnp.inf)) * dt_t[h:h + 1, :]

                lhs = jnp.concatenate([scores(he), scores(ho)], axis=1).astype(BF16)
                y_diag = jnp.dot(lhs, rhs, preferred_element_type=F32)
                e_pair = jnp.where(lo_half, e_acs[:, he:he + 1], e_acs[:, ho:ho + 1])
                y_off = cs[:, V7X_LANES * jj:V7X_LANES * (jj + 1)] * e_pair
                y_tiles.append(y_diag + y_off + xp * dsk_ref[:, V7X_LANES * j:V7X_LANES * (j + 1)])

                lhs_s = jnp.concatenate([bg_t * w_t[he:he + 1, :], bg_t * w_t[ho:ho + 1, :]],
                                        axis=1).astype(BF16)
                s_new = jnp.dot(lhs_s, rhs, preferred_element_type=F32)
                dec = jnp.where(lo_half[0:1], e_tot[:, he:he + 1], e_tot[:, ho:ho + 1])
                state_ref[g, :, V7X_LANES * jj:V7X_LANES * (jj + 1)] = (
                    s_prev[:, V7X_LANES * jj:V7X_LANES * (jj + 1)] * dec + s_new)

        y = jnp.concatenate(y_tiles, axis=1)
        gated = y * _silu(z_ref[0, rows, :])
        outs = []
        for g in range(SSD_GROUPS):
            gg = gated[:, gw * g:gw * (g + 1)]
            ms = jnp.mean(gg * gg, axis=-1, keepdims=True)
            outs.append(gg * lax.rsqrt(ms + EPS))
        y_ref[0, rows, :] = (jnp.concatenate(outs, axis=1) * ng_ref[...]).astype(y_ref.dtype)


def _ssd(xbc, z, dt_raw, dt_bias, a_log, d_skip, norm_g):
    b, s, _ = xbc.shape
    chunks = SSD_CHUNKS_PER_STEP
    ts = chunks * SSD_CHUNK
    tok = lambda wd: pl.BlockSpec((1, ts, wd), lambda i, t: (i, t, 0))
    par = lambda r, wd: pl.BlockSpec((r, wd), lambda i, t: (0, 0))
    nbytes = 2 * ts * (SSD_CONV_DIM + SSD_WIDTH + V7X_LANES) * 4 + 2 * ts * SSD_WIDTH * 2 \
        + SSD_GROUPS * SSD_STATE * SSD_WIDTH * 2 + 8 * SSD_CHUNK * SSD_CONV_DIM * 4
    return pl.pallas_call(
        functools.partial(_ssd_kernel, chunks=chunks),
        grid=(b, s // ts),
        in_specs=[tok(SSD_CONV_DIM), tok(SSD_WIDTH), tok(V7X_LANES),
                  par(1, V7X_LANES), par(1, V7X_LANES), par(1, SSD_WIDTH), par(1, SSD_WIDTH)],
        out_specs=tok(SSD_WIDTH),
        out_shape=jax.ShapeDtypeStruct((b, s, SSD_WIDTH), BF16),
        scratch_shapes=[pltpu.VMEM((SSD_GROUPS, SSD_STATE, SSD_WIDTH // SSD_GROUPS), F32)],
        compiler_params=_params(("parallel", "arbitrary"), nbytes),
        name="ssd",
    )(xbc, z, dt_raw, dt_bias, a_log, d_skip, norm_g)


def _lru_kernel(x_ref, z_ref, wg_ref, bg_ref, lam_ref, y_ref, a_ref, b_ref, h_ref):
    ts = x_ref.shape[1]
    groups = ts // V7X_SUBLANES
    first_tile = pl.program_id(1) == 0

    @pl.when(first_tile)
    def _():
        h_ref[...] = jnp.zeros_like(h_ref)

    xc = x_ref[0]
    gates = jnp.dot(xc.astype(BF16), wg_ref[...], preferred_element_type=F32) + bg_ref[...]
    r = _sigmoid(gates[:, :LRU_WIDTH])
    i = _sigmoid(gates[:, LRU_WIDTH:])
    a = jnp.exp2(r * ((-LRU_C * LOG2E) * _softplus(-lam_ref[...])))
    gap = 1.0 - a * a
    mult = jnp.where(gap > 0.0, gap * lax.rsqrt(gap), 0.0)
    seq_start = first_tile & (lax.broadcasted_iota(jnp.int32, (ts, LRU_WIDTH), 0) == 0)
    mult = jnp.where(seq_start, 1.0, mult)
    b = mult * (i * xc)

    aa = a.reshape(groups, V7X_SUBLANES, LRU_WIDTH)
    bb = b.reshape(groups, V7X_SUBLANES, LRU_WIDTH)
    sub = lax.broadcasted_iota(jnp.int32, (groups, V7X_SUBLANES, LRU_WIDTH), 1)
    for d in (1, 2, 4):
        keep = sub >= d
        bb = jnp.where(keep, aa * pltpu.roll(bb, d, axis=1) + bb, bb)
        aa = jnp.where(keep, aa * pltpu.roll(aa, d, axis=1), aa)
    a_ref[...] = aa.reshape(ts, LRU_WIDTH)
    b_ref[...] = bb.reshape(ts, LRU_WIDTH)

    def group(gi, h):
        r0 = pl.multiple_of(gi * V7X_SUBLANES, V7X_SUBLANES)
        rows = pl.ds(r0, V7X_SUBLANES)
        hh = a_ref[rows, :] * h + b_ref[rows, :]
        b_ref[rows, :] = hh
        return hh[V7X_SUBLANES - 1:V7X_SUBLANES, :]

    h_ref[...] = lax.fori_loop(0, groups, group, h_ref[...], unroll=LRU_UNROLL)
    y_ref[0] = (b_ref[...] * _silu(z_ref[0])).astype(y_ref.dtype)


def _lru(x_lru, z_lru, w_gate, b_gate, lam):
    b, s, _ = x_lru.shape
    ts = TS_LRU
    tok = pl.BlockSpec((1, ts, LRU_WIDTH), lambda i, t: (i, t, 0))
    par = lambda r, wd: pl.BlockSpec((r, wd), lambda i, t: (0, 0))
    nbytes = 2 * 2 * ts * LRU_WIDTH * 4 + 2 * ts * LRU_WIDTH * 2 + 2 * LRU_WIDTH * 2 * LRU_WIDTH * 2 \
        + 2 * ts * LRU_WIDTH * 4 + 8 * ts * LRU_WIDTH * 4
    return pl.pallas_call(
        _lru_kernel,
        grid=(b, s // ts),
        in_specs=[tok, tok, par(LRU_WIDTH, 2 * LRU_WIDTH), par(1, 2 * LRU_WIDTH), par(1, LRU_WIDTH)],
        out_specs=tok,
        out_shape=jax.ShapeDtypeStruct((b, s, LRU_WIDTH), BF16),
        scratch_shapes=[pltpu.VMEM((ts, LRU_WIDTH), F32),
                        pltpu.VMEM((ts, LRU_WIDTH), F32),
                        pltpu.VMEM((1, LRU_WIDTH), F32)],
        compiler_params=_params(("parallel", "arbitrary"), nbytes),
        name="lru",
    )(x_lru, z_lru, w_gate, b_gate, lam)


def _outproj_kernel(h_ref, ya_ref, ys_ref, yl_ref, w_ref, o_ref):
    a0, a1, a2 = ATT_WIDTH, ATT_WIDTH + SSD_WIDTH, MIX_WIDTH
    o_ref[...] = (h_ref[...]
                  + jnp.dot(ya_ref[...], w_ref[0:a0, :], preferred_element_type=F32)
                  + jnp.dot(ys_ref[...], w_ref[a0:a1, :], preferred_element_type=F32)
                  + jnp.dot(yl_ref[...], w_ref[a1:a2, :], preferred_element_type=F32))


def _outproj(h, y_att, y_ssd, y_lru, w):
    t = h.shape[0]
    tm = TM_OUTPROJ
    tok = lambda wd: pl.BlockSpec((tm, wd), lambda i: (i, 0))
    nbytes = 2 * (2 * tm * D_MODEL * 4 + tm * MIX_WIDTH * 2 + MIX_WIDTH * D_MODEL * 2)
    return pl.pallas_call(
        _outproj_kernel,
        grid=(t // tm,),
        in_specs=[tok(D_MODEL), tok(ATT_WIDTH), tok(SSD_WIDTH), tok(LRU_WIDTH),
                  pl.BlockSpec((MIX_WIDTH, D_MODEL), lambda i: (0, 0))],
        out_specs=tok(D_MODEL),
        out_shape=jax.ShapeDtypeStruct((t, D_MODEL), F32),
        compiler_params=_params(("parallel",), nbytes),
        name="outproj",
    )(h, y_att, y_ssd, y_lru, w)


def _pad_lanes(v, lane0=0, width=V7X_LANES):
    return jnp.pad(v, (lane0, width - lane0 - v.shape[0]))[None, :]


def _regroup_w_in(w):
    parts = dict(zip(IN_NAMES, jnp.split(w.astype(BF16),
                                         [sum(IN_SIZES[:i]) for i in range(1, len(IN_SIZES))], axis=-1)))
    fdt = V7X_LANES - IN_SIZES[IN_NAMES.index("f_raw")] - IN_SIZES[IN_NAMES.index("dt_raw")]
    parts["fdt"] = jnp.concatenate([parts["f_raw"], parts["dt_raw"],
                                    jnp.zeros(w.shape[:-1] + (fdt,), BF16)], axis=-1)
    return jnp.concatenate([parts[name] for name, _ in _SEGS], axis=-1)


def _block_diag(w):
    nl, nb, d, e = w.shape
    eye = jnp.eye(nb, dtype=w.dtype)
    return jnp.einsum("lnde,nm->lndme", w, eye).reshape(nl, nb * d, nb * e)


def kernel(x, norm_g, w_in, q_norm_g, k_norm_g, forget_b, ssd_conv_w, ssd_conv_b, ssd_dt_bias,
           ssd_a_log, ssd_d, ssd_norm_g, lru_conv_w, lru_conv_b, lru_w_a, lru_b_a, lru_w_x, lru_b_x,
           lru_lambda, w_out):
    bsz, seq, d = x.shape
    assert d == D_MODEL and seq % max(TQ_ATT, TM_INPROJ, TS_LRU, SSD_CHUNKS_PER_STEP * SSD_CHUNK) == 0
    assert (bsz * seq) % TM_OUTPROJ == 0
    depth = w_in.shape[0]
    h = x.astype(F32).reshape(bsz * seq, D_MODEL)
    w_in_r = _regroup_w_in(w_in)
    w_out_b = w_out.astype(BF16)
    w_gate_all = jnp.concatenate([_block_diag(lru_w_a), _block_diag(lru_w_x)], axis=-1).astype(BF16)
    for l in range(depth):
        qt, ka, vt, z_att, z_ssd, xbc, x_lru, z_lru, dt_raw = _inproj(
            h.reshape(bsz, seq, D_MODEL), norm_g[l][None, :], w_in_r[l],
            jnp.tile(q_norm_g[l], 2)[None, :], jnp.tile(k_norm_g[l], 2)[None, :],
            _pad_lanes(forget_b[l]), ssd_conv_w[l], ssd_conv_b[l][None, :],
            lru_conv_w[l], lru_conv_b[l][None, :])
        y_att = _attention(qt, ka, vt, z_att)
        y_ssd = _ssd(xbc, z_ssd, dt_raw, _pad_lanes(ssd_dt_bias[l], SSD_DT_LANE0),
                     _pad_lanes(ssd_a_log[l], SSD_DT_LANE0),
                     jnp.repeat(ssd_d[l], SSD_HEAD_DIM)[None, :], ssd_norm_g[l][None, :])
        b_gate = jnp.concatenate([lru_b_a[l], lru_b_x[l]])[None, :]
        y_lru = _lru(x_lru, z_lru, w_gate_all[l], b_gate, lru_lambda[l][None, :])
        h = _outproj(h, y_att.reshape(bsz * seq, ATT_WIDTH), y_ssd.reshape(bsz * seq, SSD_WIDTH),
                     y_lru.reshape(bsz * seq, LRU_WIDTH), w_out_b[l])
    return h.reshape(bsz, seq, D_MODEL).astype(x.dtype)
```

```python
import functools

import jax
import jax.numpy as jnp
from jax import lax
from jax.experimental import pallas as pl
from jax.experimental.pallas import tpu as pltpu

F32 = jnp.float32
BF16 = jnp.bfloat16

V7X_LANES = 128
V7X_SUBLANES = 8
V7X_VMEM_BYTES = 64 * 1024 * 1024

D_MODEL = 1024
EPS = 1e-6
CONV_WIDTH = 4
MIX_WIDTH = 2 * D_MODEL
ATT_HEAD_DIM = 64
ATT_WIDTH = MIX_WIDTH // 4
ATT_HEADS = ATT_WIDTH // ATT_HEAD_DIM
SSD_HEAD_DIM = 64
SSD_WIDTH = MIX_WIDTH // 2
SSD_HEADS = SSD_WIDTH // SSD_HEAD_DIM
SSD_GROUPS = 2
SSD_STATE = 128
SSD_CHUNK = 128
SSD_CONV_DIM = SSD_WIDTH + 2 * SSD_GROUPS * SSD_STATE
LRU_WIDTH = MIX_WIDTH // 4
LRU_BLOCKS = 8
LRU_C = 8.0
IN_SIZES = (ATT_WIDTH, ATT_WIDTH, ATT_WIDTH, ATT_WIDTH, ATT_HEADS,
            SSD_WIDTH, SSD_CONV_DIM, SSD_HEADS, LRU_WIDTH, LRU_WIDTH)
IN_NAMES = ("q", "k", "v", "z_att", "f_raw", "z_ssd", "xbc", "dt_raw", "x_lru", "z_lru")

LOG2E = 1.4426950408889634
HALO = V7X_SUBLANES

SSD_DT_LANE0 = ATT_HEADS
_SEGS = (("q", ATT_WIDTH), ("k", ATT_WIDTH), ("v", ATT_WIDTH), ("z_att", ATT_WIDTH),
         ("z_ssd", SSD_WIDTH), ("xbc", SSD_CONV_DIM), ("x_lru", LRU_WIDTH), ("z_lru", LRU_WIDTH),
         ("fdt", V7X_LANES))
_SEG = {}
for _name, _width in _SEGS:
    _SEG[_name] = (sum(w for _, w in _SEGS[:len(_SEG)]), sum(w for _, w in _SEGS[:len(_SEG)]) + _width)
D_IN_PAD = sum(w for _, w in _SEGS)

TQ_ATT = 512
TK_ATT = 256
ATT_VT_ROWS = 80
ATT_HEADS_PER_STEP = 8
ATT_LOOKAHEAD = 3
TM_INPROJ = 256
INPROJ_CHUNK = 256
SSD_CHUNKS_PER_STEP = 2
TS_LRU = 1024
LRU_UNROLL = 4
TM_OUTPROJ = 1024


def _vmem_limit(nbytes):
    return int(min(nbytes * 3 // 2 + (8 << 20), V7X_VMEM_BYTES - (6 << 20)))


def _params(semantics, nbytes):
    return pltpu.CompilerParams(dimension_semantics=semantics, vmem_limit_bytes=_vmem_limit(nbytes))


def _sigmoid(x):
    return 1.0 / (1.0 + jnp.exp(-x))


def _silu(x):
    return x * _sigmoid(x)


def _softplus(x):
    return jnp.maximum(x, 0.0) + jnp.log(1.0 + jnp.exp(-jnp.abs(x)))


def _split3(x):
    hi = x.astype(BF16)
    r1 = x - hi.astype(F32)
    mid = r1.astype(BF16)
    lo = (r1 - mid.astype(F32)).astype(BF16)
    return hi, mid, lo


def _tril_ones(n):
    row = lax.broadcasted_iota(jnp.int32, (n, n), 0)
    col = lax.broadcasted_iota(jnp.int32, (n, n), 1)
    return jnp.where(col <= row, 1.0, 0.0).astype(BF16)


def _cumsum_rows(x, tri):
    hi, mid, lo = _split3(x)
    dot = lambda a: jnp.dot(tri, a, preferred_element_type=F32)
    return dot(hi) + dot(mid) + dot(lo)


def _causal_conv(pad_ref, w_ref, b_ref, cols):
    rows = pad_ref.shape[0] - HALO
    xp = pad_ref[:, cols]
    x = xp[HALO:, :]
    y = b_ref[:, cols] + w_ref[CONV_WIDTH - 1:CONV_WIDTH, cols] * x
    for shift in range(1, CONV_WIDTH):
        kk = CONV_WIDTH - 1 - shift
        y = y + w_ref[kk:kk + 1, cols] * pltpu.roll(xp, shift, axis=0)[HALO:, :]
    pad_ref[0:HALO, cols] = x[rows - HALO:rows, :]
    return y


def _forget_bias_terms(qkvf_ref, fb, cum_ref, bias_ref):
    ts = qkvf_ref.shape[0]
    log_f = -_softplus(-(qkvf_ref[:, 3 * ATT_WIDTH:] + fb))
    c = cum_ref[...] + _cumsum_rows(log_f, _tril_ones(ts))
    cum_ref[...] = c[ts - 1:ts, :]
    for t, term in enumerate(_split3(c * (-LOG2E))):
        bias_ref[t] = term.astype(F32)


def _att_operands_pair(j, qkvf_ref, bias_ref, qg, kg, qt_ref, ka_ref, vt_ref):
    ts = qkvf_ref.shape[0]
    lane = lax.broadcasted_iota(jnp.int32, (ts, V7X_LANES), 1)
    lo_half = lane < ATT_HEAD_DIM
    q_scale = ATT_HEAD_DIM ** -0.5 * LOG2E

    def normed(x, g):
        sq = x * x
        ss_lo = jnp.sum(jnp.where(lo_half, sq, 0.0), axis=1, keepdims=True)
        ss_hi = jnp.sum(jnp.where(lo_half, 0.0, sq), axis=1, keepdims=True)
        inv = jnp.where(lo_half, lax.rsqrt(ss_lo / ATT_HEAD_DIM + EPS),
                        lax.rsqrt(ss_hi / ATT_HEAD_DIM + EPS))
        return x * inv * g

    lo, hi = V7X_LANES * j, V7X_LANES * (j + 1)
    qn = normed(qkvf_ref[:, lo:hi], qg) * q_scale
    kn = normed(qkvf_ref[:, ATT_WIDTH + lo:ATT_WIDTH + hi], kg)
    vp_t = qkvf_ref[:, 2 * ATT_WIDTH + lo:2 * ATT_WIDTH + hi].T
    pad_rows = ATT_VT_ROWS - ATT_HEAD_DIM
    ones_row = jnp.where(lax.broadcasted_iota(jnp.int32, (pad_rows, ts), 0) == 0, 1.0, 0.0)
    vt_ref[0, 2 * j, 0] = jnp.concatenate([vp_t[:ATT_HEAD_DIM, :], ones_row], axis=0).astype(BF16)
    vt_ref[0, 2 * j + 1, 0] = jnp.concatenate([ones_row, vp_t[ATT_HEAD_DIM:, :]], axis=0).astype(BF16)
    qn_t = qn.T
    ones3_rows = jnp.where(lax.broadcasted_iota(jnp.int32, (ATT_HEAD_DIM, ts), 0) < 3, 1.0, 0.0)
    qt_ref[0, 2 * j, 0] = jnp.concatenate([qn_t[:ATT_HEAD_DIM, :], ones3_rows], axis=0).astype(BF16)
    qt_ref[0, 2 * j + 1, 0] = jnp.concatenate([qn_t[ATT_HEAD_DIM:, :], ones3_rows], axis=0).astype(BF16)
    for parity in range(2):
        h = 2 * j + parity
        kh = kn if parity == 0 else pltpu.roll(kn, ATT_HEAD_DIM, axis=1)
        bias = jnp.where(lane == ATT_HEAD_DIM, bias_ref[0, :, h:h + 1],
                         jnp.where(lane == ATT_HEAD_DIM + 1, bias_ref[1, :, h:h + 1],
                                   jnp.where(lane == ATT_HEAD_DIM + 2, bias_ref[2, :, h:h + 1], 0.0)))
        ka_ref[0, h] = jnp.where(lo_half, kh, bias).astype(BF16)


def _inproj_kernel(x_ref, g_ref, w_ref, qg_ref, kg_ref, fb_ref, scw_ref, scb_ref, lcw_ref, lcb_ref,
                   qt_ref, ka_ref, vt_ref, zatt_ref, zssd_ref, xbc_ref, xlru_ref, zlru_ref, dt_ref,
                   cum_ref, spad_ref, lpad_ref, qkvf_ref, bias_ref):
    tm = x_ref.shape[1]

    @pl.when(pl.program_id(1) == 0)
    def _():
        cum_ref[...] = jnp.zeros_like(cum_ref)
        spad_ref[0:HALO, :] = jnp.zeros((HALO, SSD_CONV_DIM), F32)
        lpad_ref[0:HALO, :] = jnp.zeros((HALO, LRU_WIDTH), F32)

    x = x_ref[0]
    ms = jnp.mean(x * x, axis=-1, keepdims=True)
    u = (x * lax.rsqrt(ms + EPS) * g_ref[...]).astype(BF16)

    def proj(name, cols=None):
        lo, hi = _SEG[name]
        if cols is not None:
            lo, hi = lo + cols.start, lo + cols.stop
        return jnp.dot(u, w_ref[:, lo:hi], preferred_element_type=F32)

    chunk = lambda c: slice(c * INPROJ_CHUNK, (c + 1) * INPROJ_CHUNK)
    plain = [(ref, name, chunk(c))
             for ref, name, width in ((zssd_ref, "z_ssd", SSD_WIDTH), (zatt_ref, "z_att", ATT_WIDTH),
                                      (zlru_ref, "z_lru", LRU_WIDTH))
             for c in range(width // INPROJ_CHUNK)]
    convs = [(spad_ref, scw_ref, scb_ref, xbc_ref, "xbc", chunk(c), True)
             for c in range(SSD_CONV_DIM // INPROJ_CHUNK)]
    convs += [(lpad_ref, lcw_ref, lcb_ref, xlru_ref, "x_lru", chunk(c), False)
              for c in range(LRU_WIDTH // INPROJ_CHUNK)]

    def matmul_plain():
        if plain:
            ref, name, cols = plain.pop(0)
            ref[0, :, cols] = proj(name, cols)

    def matmul_conv(ci):
        if ci < len(convs):
            pad_ref, _, _, _, name, cols, _ = convs[ci]
            pad_ref[HALO:HALO + tm, cols] = proj(name, cols)

    def finish_conv(ci):
        pad_ref, cw_ref, cb_ref, out_ref, _, cols, silu = convs[ci]
        y = _causal_conv(pad_ref, cw_ref, cb_ref, cols)
        out_ref[0, :, cols] = _silu(y) if silu else y

    for si, name in enumerate(("q", "k", "v")):
        qkvf_ref[:, si * ATT_WIDTH:(si + 1) * ATT_WIDTH] = proj(name)
    fdt = proj("fdt")
    qkvf_ref[:, 3 * ATT_WIDTH:] = fdt
    dt_ref[0] = fdt
    matmul_conv(0)
    matmul_plain()
    _forget_bias_terms(qkvf_ref, fb_ref[...], cum_ref, bias_ref)
    pairs = list(range(ATT_HEADS // 2))
    for ci in range(len(convs)):
        matmul_conv(ci + 1)
        if pairs:
            _att_operands_pair(pairs.pop(0), qkvf_ref, bias_ref, qg_ref[...], kg_ref[...],
                               qt_ref, ka_ref, vt_ref)
        matmul_plain()
        finish_conv(ci)
    assert not pairs and not plain


def _inproj(h, g, w, qg, kg, fb, ssd_cw, ssd_cb, lru_cw, lru_cb):
    b, s, _ = h.shape
    tm = TM_INPROJ
    tok = lambda wd: pl.BlockSpec((1, tm, wd), lambda i, t: (i, t, 0))
    par = lambda r, wd: pl.BlockSpec((r, wd), lambda i, t: (0, 0))
    head = pl.BlockSpec((1, ATT_HEADS, tm, V7X_LANES), lambda i, t: (i, 0, t, 0))
    head_t = pl.BlockSpec((1, ATT_HEADS, 1, ATT_VT_ROWS, tm), lambda i, t: (i, 0, t, 0, 0))
    head_qt = pl.BlockSpec((1, ATT_HEADS, 1, V7X_LANES, tm), lambda i, t: (i, 0, t, 0, 0))
    tok_widths = (ATT_WIDTH, SSD_WIDTH, SSD_CONV_DIM, LRU_WIDTH, LRU_WIDTH, V7X_LANES)
    nbytes = 2 * (tm * D_MODEL * 4 + D_MODEL * D_IN_PAD * 2 + tm * sum(tok_widths) * 4
                  + 3 * ATT_HEADS * tm * V7X_LANES * 2) \
        + (HALO + tm) * (SSD_CONV_DIM + LRU_WIDTH) * 4 + 4 * tm * SSD_CONV_DIM * 4
    return pl.pallas_call(
        _inproj_kernel,
        grid=(b, s // tm),
        in_specs=[tok(D_MODEL), par(1, D_MODEL), par(D_MODEL, D_IN_PAD),
                  par(1, V7X_LANES), par(1, V7X_LANES), par(1, V7X_LANES),
                  par(CONV_WIDTH, SSD_CONV_DIM), par(1, SSD_CONV_DIM),
                  par(CONV_WIDTH, LRU_WIDTH), par(1, LRU_WIDTH)],
        out_specs=[head_qt, head, head_t] + [tok(wd) for wd in tok_widths],
        out_shape=[jax.ShapeDtypeStruct((b, ATT_HEADS, s // tm, V7X_LANES, tm), BF16),
                   jax.ShapeDtypeStruct((b, ATT_HEADS, s, V7X_LANES), BF16),
                   jax.ShapeDtypeStruct((b, ATT_HEADS, s // tm, ATT_VT_ROWS, tm), BF16)]
        + [jax.ShapeDtypeStruct((b, s, wd), F32) for wd in tok_widths],
        scratch_shapes=[pltpu.VMEM((1, V7X_LANES), F32),
                        pltpu.VMEM((HALO + tm, SSD_CONV_DIM), F32),
                        pltpu.VMEM((HALO + tm, LRU_WIDTH), F32),
                        pltpu.VMEM((tm, 3 * ATT_WIDTH + V7X_LANES), F32),
                        pltpu.VMEM((3, tm, V7X_LANES), F32)],
        compiler_params=_params(("parallel", "arbitrary"), nbytes),
        name="inproj",
    )(h, g, w, qg, kg, fb, ssd_cw, ssd_cb, lru_cw, lru_cb)


def _att_kernel(qt_ref, ka_ref, vt_ref, z_ref, o_ref, s_ref, cmax_ref, m_ref, acc_ref,
                *, tq, tk, tv, heads):
    qi = pl.program_id(2)
    subs = tq // tk
    tqb = qt_ref.shape[-1]
    assert tk % tqb == 0
    chains = [(sub, hh) for sub in range(subs) for hh in range(heads)]
    slots = ATT_LOOKAHEAD + 1

    m_ref[...] = jnp.full(m_ref.shape, -jnp.inf, F32)
    acc_ref[...] = jnp.zeros(acc_ref.shape, F32)

    assert len(chains) % slots == 0

    def causal_mask(s_t, sub, q0):
        key_r = lax.broadcasted_iota(jnp.int32, s_t.shape, 0)
        qry_c = lax.broadcasted_iota(jnp.int32, s_t.shape, 1)
        return jnp.where(key_r + (sub * tk - q0) <= qry_c, s_t, -jnp.inf)

    def issue_logits(j, idx, q0=0, diagonal=False):
        sub, hh = chains[idx]
        k0 = pl.multiple_of((j * subs + sub) * tk, tk)
        keys = ka_ref[0, hh, pl.ds(k0, tk), :]
        for qb in range(q0 // tqb, tq // tqb):
            cols = slice(qb * tqb, (qb + 1) * tqb)
            s_t = jnp.dot(keys, qt_ref[0, hh, qb], preferred_element_type=F32)
            if diagonal:
                s_t = causal_mask(s_t, sub, qb * tqb)
            s_ref[idx % slots, :, cols] = s_t
            cmax_ref[idx % slots, :, cols] = jnp.max(s_t, axis=0, keepdims=True)

    def step(j, diagonal):
        for idx, (sub, hh) in enumerate(chains):
            q0 = sub * tk if diagonal else 0
            s_t = s_ref[idx % slots, :, q0:]
            cmax = cmax_ref[idx % slots, :, q0:]
            ahead = idx + ATT_LOOKAHEAD
            if ahead < len(chains):
                issue_logits(j, ahead, chains[ahead][0] * tk if diagonal else 0, diagonal)
            elif not diagonal:
                issue_logits(j + 1, ahead - len(chains))
            if diagonal and idx < ATT_LOOKAHEAD:
                s_t = causal_mask(s_t, sub, q0)
                cmax = jnp.max(s_t, axis=0, keepdims=True)
            m = m_ref[hh, :, q0:]
            m_new = jnp.maximum(m, cmax)
            m_ref[hh, :, q0:] = m_new
            p_t = jnp.exp2(s_t - m_new).astype(BF16)
            acc = acc_ref[hh, :, q0:] * jnp.exp2(m - m_new)
            for vb in range(tk // tv):
                acc = acc + jnp.dot(vt_ref[0, hh, (j * subs + sub) * (tk // tv) + vb],
                                    p_t[vb * tv:(vb + 1) * tv, :], preferred_element_type=F32)
            acc_ref[hh, :, q0:] = acc

    for idx in range(ATT_LOOKAHEAD):
        issue_logits(0, idx)

    @pl.loop(0, qi)
    def _(j):
        step(j, False)

    step(qi, True)

    lead = ATT_VT_ROWS - ATT_HEAD_DIM
    for pr in range(heads // 2):
        even, odd = acc_ref[2 * pr], acc_ref[2 * pr + 1]
        o_t = jnp.concatenate(
            [even[0:ATT_HEAD_DIM, :] * (1.0 / even[ATT_HEAD_DIM:ATT_HEAD_DIM + 1, :]),
             odd[lead:, :] * (1.0 / odd[0:1, :])], axis=0)
        lanes = slice(V7X_LANES * pr, V7X_LANES * (pr + 1))
        o_ref[0, :, lanes] = (o_t.T * _silu(z_ref[0, :, lanes])).astype(o_ref.dtype)


def _attention(qt, ka, vt, z_att):
    b, _, s, _ = ka.shape
    tq, tk, hp = TQ_ATT, TK_ATT, ATT_HEADS_PER_STEP
    assert tq % tk == 0 and hp % 2 == 0 and ATT_HEADS % hp == 0
    wd = hp * ATT_HEAD_DIM
    tqb = qt.shape[-1]
    q_tile = pl.BlockSpec((1, hp, tq // tqb, V7X_LANES, tqb), lambda i, j, t: (i, j, t, 0, 0))
    k_full = pl.BlockSpec((1, hp, s, V7X_LANES), lambda i, j, t: (i, j, 0, 0))
    tv = vt.shape[-1]
    assert tk % tv == 0
    v_full = pl.BlockSpec((1, hp, s // tv, ATT_VT_ROWS, tv), lambda i, j, t: (i, j, 0, 0, 0))
    tok = pl.BlockSpec((1, tq, wd), lambda i, j, t: (i, t, j))
    nbytes = 2 * (hp * tq * V7X_LANES * 2 + 2 * hp * s * V7X_LANES * 2 + tq * wd * 6) \
        + (2 * ATT_LOOKAHEAD + 4) * tq * tk * 4
    return pl.pallas_call(
        functools.partial(_att_kernel, tq=tq, tk=tk, tv=tv, heads=hp),
        grid=(b, ATT_HEADS // hp, s // tq),
        in_specs=[q_tile, k_full, v_full, tok],
        out_specs=tok,
        out_shape=jax.ShapeDtypeStruct((b, s, ATT_WIDTH), BF16),
        scratch_shapes=[pltpu.VMEM((ATT_LOOKAHEAD + 1, tk, tq), F32),
                        pltpu.VMEM((ATT_LOOKAHEAD + 1, 1, tq), F32),
                        pltpu.VMEM((hp, 1, tq), F32),
                        pltpu.VMEM((hp, ATT_VT_ROWS, tq), F32)],
        compiler_params=_params(("parallel", "parallel", "arbitrary"), nbytes),
        name="att",
    )(qt, ka, vt, z_att)


def _ssd_kernel(xbc_ref, z_ref, dt_ref, dtb_ref, alog_ref, dsk_ref, ng_ref, y_ref, state_ref,
                *, chunks):
    L = SSD_CHUNK
    n = SSD_STATE
    gw = SSD_WIDTH // SSD_GROUPS

    @pl.when(pl.program_id(1) == 0)
    def _():
        state_ref[...] = jnp.zeros_like(state_ref)

    row = lax.broadcasted_iota(jnp.int32, (L, L), 0)
    col = lax.broadcasted_iota(jnp.int32, (L, L), 1)
    causal = col <= row
    tri = jnp.where(causal, 1.0, 0.0).astype(BF16)
    lo_half = lax.broadcasted_iota(jnp.int32, (L, V7X_LANES), 1) < SSD_HEAD_DIM
    a_neg = -jnp.exp(alog_ref[...])

    def decay_terms(ci):
        rows = slice(ci * L, (ci + 1) * L)
        dt = _softplus(dt_ref[0, rows, :] + dtb_ref[...])
        acs = _cumsum_rows(dt * a_neg, tri)
        tot = acs[L - 1:L, :]
        return (acs, jnp.exp(acs), jnp.exp(tot), acs.T, dt.T, (dt * jnp.exp(tot - acs)).T)

    terms = decay_terms(0)
    for ci in range(chunks):
        rows = slice(ci * L, (ci + 1) * L)
        acs, e_acs, e_tot, acs_t, dt_t, w_t = terms
        if ci + 1 < chunks:
            terms = decay_terms(ci + 1)

        y_tiles = []
        for g in range(SSD_GROUPS):
            b_lo = SSD_WIDTH + n * g
            c_lo = SSD_WIDTH + SSD_GROUPS * n + n * g
            bg_t = xbc_ref[0, rows, b_lo:b_lo + n].T
            cg = xbc_ref[0, rows, c_lo:c_lo + n].astype(BF16)
            cb = jnp.dot(cg, bg_t.astype(BF16), preferred_element_type=F32)
            s_prev = state_ref[g]
            cs = jnp.dot(cg, s_prev.astype(BF16), preferred_element_type=F32)
            for jj in range(gw // V7X_LANES):
                j = g * (gw // V7X_LANES) + jj
                he, ho = SSD_DT_LANE0 + 2 * j, SSD_DT_LANE0 + 2 * j + 1
                xp = xbc_ref[0, rows, V7X_LANES * j:V7X_LANES * (j + 1)]
                rhs = jnp.concatenate([jnp.where(lo_half, xp, 0.0), jnp.where(lo_half, 0.0, xp)],
                                      axis=0).astype(BF16)

                def scores(h):
                    seg = acs[:, h:h + 1] - acs_t[h:h + 1, :]
                    return cb * jnp.exp(jnp.where(causal, seg, -jnp.inf)) * dt_t[h:h + 1, :]

                lhs = jnp.concatenate([scores(he), scores(ho)], axis=1).astype(BF16)
                y_diag = jnp.dot(lhs, rhs, preferred_element_type=F32)
                e_pair = jnp.where(lo_half, e_acs[:, he:he + 1], e_acs[:, ho:ho + 1])
                y_off = cs[:, V7X_LANES * jj:V7X_LANES * (jj + 1)] * e_pair
                y_tiles.append(y_diag + y_off + xp * dsk_ref[:, V7X_LANES * j:V7X_LANES * (j + 1)])

                lhs_s = jnp.concatenate([bg_t * w_t[he:he + 1, :], bg_t * w_t[ho:ho + 1, :]],
                                        axis=1).astype(BF16)
                s_new = jnp.dot(lhs_s, rhs, preferred_element_type=F32)
                dec = jnp.where(lo_half[0:1], e_tot[:, he:he + 1], e_tot[:, ho:ho + 1])
                state_ref[g, :, V7X_LANES * jj:V7X_LANES * (jj + 1)] = (
                    s_prev[:, V7X_LANES * jj:V7X_LANES * (jj + 1)] * dec + s_new)

        y = jnp.concatenate(y_tiles, axis=1)
        gated = y * _silu(z_ref[0, rows, :])
        outs = []
        for g in range(SSD_GROUPS):
            gg = gated[:, gw * g:gw * (g + 1)]
            ms = jnp.mean(gg * gg, axis=-1, keepdims=True)
            outs.append(gg * lax.rsqrt(ms + EPS))
        y_ref[0, rows, :] = (jnp.concatenate(outs, axis=1) * ng_ref[...]).astype(y_ref.dtype)


def _ssd(xbc, z, dt_raw, dt_bias, a_log, d_skip, norm_g):
    b, s, _ = xbc.shape
    chunks = SSD_CHUNKS_PER_STEP
    ts = chunks * SSD_CHUNK
    tok = lambda wd: pl.BlockSpec((1, ts, wd), lambda i, t: (i, t, 0))
    par = lambda r, wd: pl.BlockSpec((r, wd), lambda i, t: (0, 0))
    nbytes = 2 * ts * (SSD_CONV_DIM + SSD_WIDTH + V7X_LANES) * 4 + 2 * ts * SSD_WIDTH * 2 \
        + SSD_GROUPS * SSD_STATE * SSD_WIDTH * 2 + 8 * SSD_CHUNK * SSD_CONV_DIM * 4
    return pl.pallas_call(
        functools.partial(_ssd_kernel, chunks=chunks),
        grid=(b, s // ts),
        in_specs=[tok(SSD_CONV_DIM), tok(SSD_WIDTH), tok(V7X_LANES),
                  par(1, V7X_LANES), par(1, V7X_LANES), par(1, SSD_WIDTH), par(1, SSD_WIDTH)],
        out_specs=tok(SSD_WIDTH),
        out_shape=jax.ShapeDtypeStruct((b, s, SSD_WIDTH), BF16),
        scratch_shapes=[pltpu.VMEM((SSD_GROUPS, SSD_STATE, SSD_WIDTH // SSD_GROUPS), F32)],
        compiler_params=_params(("parallel", "arbitrary"), nbytes),
        name="ssd",
    )(xbc, z, dt_raw, dt_bias, a_log, d_skip, norm_g)


def _lru_kernel(x_ref, z_ref, wg_ref, bg_ref, lam_ref, y_ref, a_ref, b_ref, h_ref):
    ts = x_ref.shape[1]
    groups = ts // V7X_SUBLANES
    first_tile = pl.program_id(1) == 0

    @pl.when(first_tile)
    def _():
        h_ref[...] = jnp.zeros_like(h_ref)

    xc = x_ref[0]
    gates = jnp.dot(xc.astype(BF16), wg_ref[...], preferred_element_type=F32) + bg_ref[...]
    r = _sigmoid(gates[:, :LRU_WIDTH])
    i = _sigmoid(gates[:, LRU_WIDTH:])
    a = jnp.exp2(r * ((-LRU_C * LOG2E) * _softplus(-lam_ref[...])))
    gap = 1.0 - a * a
    mult = jnp.where(gap > 0.0, gap * lax.rsqrt(gap), 0.0)
    seq_start = first_tile & (lax.broadcasted_iota(jnp.int32, (ts, LRU_WIDTH), 0) == 0)
    mult = jnp.where(seq_start, 1.0, mult)
    b = mult * (i * xc)

    aa = a.reshape(groups, V7X_SUBLANES, LRU_WIDTH)
    bb = b.reshape(groups, V7X_SUBLANES, LRU_WIDTH)
    sub = lax.broadcasted_iota(jnp.int32, (groups, V7X_SUBLANES, LRU_WIDTH), 1)
    for d in (1, 2, 4):
        keep = sub >= d
        bb = jnp.where(keep, aa * pltpu.roll(bb, d, axis=1) + bb, bb)
        aa = jnp.where(keep, aa * pltpu.roll(aa, d, axis=1), aa)
    a_ref[...] = aa.reshape(ts, LRU_WIDTH)
    b_ref[...] = bb.reshape(ts, LRU_WIDTH)

    def group(gi, h):
        r0 = pl.multiple_of(gi * V7X_SUBLANES, V7X_SUBLANES)
        rows = pl.ds(r0, V7X_SUBLANES)
        hh = a_ref[rows, :] * h + b_ref[rows, :]
        b_ref[rows, :] = hh
        return hh[V7X_SUBLANES - 1:V7X_SUBLANES, :]

    h_ref[...] = lax.fori_loop(0, groups, group, h_ref[...], unroll=LRU_UNROLL)
    y_ref[0] = (b_ref[...] * _silu(z_ref[0])).astype(y_ref.dtype)


def _lru(x_lru, z_lru, w_gate, b_gate, lam):
    b, s, _ = x_lru.shape
    ts = TS_LRU
    tok = pl.BlockSpec((1, ts, LRU_WIDTH), lambda i, t: (i, t, 0))
    par = lambda r, wd: pl.BlockSpec((r, wd), lambda i, t: (0, 0))
    nbytes = 2 * 2 * ts * LRU_WIDTH * 4 + 2 * ts * LRU_WIDTH * 2 + 2 * LRU_WIDTH * 2 * LRU_WIDTH * 2 \
        + 2 * ts * LRU_WIDTH * 4 + 8 * ts * LRU_WIDTH * 4
    return pl.pallas_call(
        _lru_kernel,
        grid=(b, s // ts),
        in_specs=[tok, tok, par(LRU_WIDTH, 2 * LRU_WIDTH), par(1, 2 * LRU_WIDTH), par(1, LRU_WIDTH)],
        out_specs=tok,
        out_shape=jax.ShapeDtypeStruct((b, s, LRU_WIDTH), BF16),
        scratch_shapes=[pltpu.VMEM((ts, LRU_WIDTH), F32),
                        pltpu.VMEM((ts, LRU_WIDTH), F32),
                        pltpu.VMEM((1, LRU_WIDTH), F32)],
        compiler_params=_params(("parallel", "arbitrary"), nbytes),
        name="lru",
    )(x_lru, z_lru, w_gate, b_gate, lam)


def _outproj_kernel(h_ref, ya_ref, ys_ref, yl_ref, w_ref, o_ref):
    a0, a1, a2 = ATT_WIDTH, ATT_WIDTH + SSD_WIDTH, MIX_WIDTH
    o_ref[...] = (h_ref[...]
                  + jnp.dot(ya_ref[...], w_ref[0:a0, :], preferred_element_type=F32)
                  + jnp.dot(ys_ref[...], w_ref[a0:a1, :], preferred_element_type=F32)
                  + jnp.dot(yl_ref[...], w_ref[a1:a2, :], preferred_element_type=F32))


def _outproj(h, y_att, y_ssd, y_lru, w):
    t = h.shape[0]
    tm = TM_OUTPROJ
    tok = lambda wd: pl.BlockSpec((tm, wd), lambda i: (i, 0))
    nbytes = 2 * (2 * tm * D_MODEL * 4 + tm * MIX_WIDTH * 2 + MIX_WIDTH * D_MODEL * 2)
    return pl.pallas_call(
        _outproj_kernel,
        grid=(t // tm,),
        in_specs=[tok(D_MODEL), tok(ATT_WIDTH), tok(SSD_WIDTH), tok(LRU_WIDTH),
                  pl.BlockSpec((MIX_WIDTH, D_MODEL), lambda i: (0, 0))],
        out_specs=tok(D_MODEL),
        out_shape=jax.ShapeDtypeStruct((t, D_MODEL), F32),
        compiler_params=_params(("parallel",), nbytes),
        name="outproj",
    )(h, y_att, y_ssd, y_lru, w)


def _pad_lanes(v, lane0=0, width=V7X_LANES):
    return jnp.pad(v, (lane0, width - lane0 - v.shape[0]))[None, :]


def _regroup_w_in(w):
    off = {name: sum(IN_SIZES[:i]) for i, name in enumerate(IN_NAMES)}
    end = sum(IN_SIZES)
    assert [name for name, _ in _SEGS[:-1]] == [n for n in IN_NAMES if n not in ("f_raw", "dt_raw")]
    pad = V7X_LANES - IN_SIZES[IN_NAMES.index("f_raw")] - IN_SIZES[IN_NAMES.index("dt_raw")]
    runs = [w[..., :off["f_raw"]], w[..., off["z_ssd"]:off["dt_raw"]], w[..., off["x_lru"]:end],
            w[..., off["f_raw"]:off["z_ssd"]], w[..., off["dt_raw"]:off["x_lru"]],
            jnp.zeros(w.shape[:-1] + (pad,), w.dtype)]
    return jnp.concatenate(runs, axis=-1).astype(BF16)


def _block_diag(w):
    nb, d, e = w.shape
    tiled = jnp.tile(w.reshape(nb * d, e), (1, nb))
    row = lax.broadcasted_iota(jnp.int32, tiled.shape, 0) // d
    col = lax.broadcasted_iota(jnp.int32, tiled.shape, 1) // e
    return jnp.where(row == col, tiled, 0.0)


def kernel(x, norm_g, w_in, q_norm_g, k_norm_g, forget_b, ssd_conv_w, ssd_conv_b, ssd_dt_bias,
           ssd_a_log, ssd_d, ssd_norm_g, lru_conv_w, lru_conv_b, lru_w_a, lru_b_a, lru_w_x, lru_b_x,
           lru_lambda, w_out):
    bsz, seq, d = x.shape
    assert d == D_MODEL and seq % max(TQ_ATT, TM_INPROJ, TS_LRU, SSD_CHUNKS_PER_STEP * SSD_CHUNK) == 0
    assert (bsz * seq) % TM_OUTPROJ == 0
    depth = w_in.shape[0]
    h = x.astype(F32).reshape(bsz * seq, D_MODEL)
    for l in range(depth):
        qt, ka, vt, z_att, z_ssd, xbc, x_lru, z_lru, dt_raw = _inproj(
            h.reshape(bsz, seq, D_MODEL), norm_g[l][None, :], _regroup_w_in(w_in[l]),
            jnp.tile(q_norm_g[l], 2)[None, :], jnp.tile(k_norm_g[l], 2)[None, :],
            _pad_lanes(forget_b[l]), ssd_conv_w[l], ssd_conv_b[l][None, :],
            lru_conv_w[l], lru_conv_b[l][None, :])
        y_att = _attention(qt, ka, vt, z_att)
        y_ssd = _ssd(xbc, z_ssd, dt_raw, _pad_lanes(ssd_dt_bias[l], SSD_DT_LANE0),
                     _pad_lanes(ssd_a_log[l], SSD_DT_LANE0),
                     jnp.repeat(ssd_d[l], SSD_HEAD_DIM)[None, :], ssd_norm_g[l][None, :])
        b_gate = jnp.concatenate([lru_b_a[l], lru_b_x[l]])[None, :]
        w_gate = jnp.concatenate([_block_diag(lru_w_a[l]), _block_diag(lru_w_x[l])], axis=1).astype(BF16)
        y_lru = _lru(x_lru, z_lru, w_gate, b_gate, lru_lambda[l][None, :])
        h = _outproj(h, y_att.reshape(bsz * seq, ATT_WIDTH), y_ssd.reshape(bsz * seq, SSD_WIDTH),
                     y_lru.reshape(bsz * seq, LRU_WIDTH), w_out[l].astype(BF16))
    return h.reshape(bsz, seq, D_MODEL).astype(x.dtype)
```

```python
import functools

import jax
import jax.numpy as jnp
from jax import lax
from jax.experimental import pallas as pl
from jax.experimental.pallas import tpu as pltpu

F32 = jnp.float32
BF16 = jnp.bfloat16

V7X_LANES = 128
V7X_SUBLANES = 8
V7X_VMEM_BYTES = 64 * 1024 * 1024

D_MODEL = 1024
EPS = 1e-6
CONV_WIDTH = 4
MIX_WIDTH = 2 * D_MODEL
ATT_HEAD_DIM = 64
ATT_WIDTH = MIX_WIDTH // 4
ATT_HEADS = ATT_WIDTH // ATT_HEAD_DIM
SSD_HEAD_DIM = 64
SSD_WIDTH = MIX_WIDTH // 2
SSD_HEADS = SSD_WIDTH // SSD_HEAD_DIM
SSD_GROUPS = 2
SSD_STATE = 128
SSD_CHUNK = 128
SSD_CONV_DIM = SSD_WIDTH + 2 * SSD_GROUPS * SSD_STATE
LRU_WIDTH = MIX_WIDTH // 4
LRU_BLOCKS = 8
LRU_C = 8.0
IN_SIZES = (ATT_WIDTH, ATT_WIDTH, ATT_WIDTH, ATT_WIDTH, ATT_HEADS,
            SSD_WIDTH, SSD_CONV_DIM, SSD_HEADS, LRU_WIDTH, LRU_WIDTH)
IN_NAMES = ("q", "k", "v", "z_att", "f_raw", "z_ssd", "xbc", "dt_raw", "x_lru", "z_lru")

LOG2E = 1.4426950408889634
HALO = V7X_SUBLANES

SSD_DT_LANE0 = ATT_HEADS
_RUNS = ((("q", ATT_WIDTH), ("k", ATT_WIDTH), ("v", ATT_WIDTH), ("z_att", ATT_WIDTH)),
         (("z_ssd", SSD_WIDTH), ("xbc", SSD_CONV_DIM)),
         (("x_lru", LRU_WIDTH), ("z_lru", LRU_WIDTH)),
         (("fdt", V7X_LANES),))
_RUN_WIDTHS = tuple(sum(w for _, w in run) for run in _RUNS)
_SEG = {name: (ri, sum(w for _, w in run[:si]), sum(w for _, w in run[:si + 1]))
        for ri, run in enumerate(_RUNS) for si, (name, _) in enumerate(run)}
D_IN_PAD = sum(_RUN_WIDTHS)

TQ_ATT = 1024
TK_ATT = 256
ATT_VT_ROWS = 80
ATT_HEADS_PER_STEP = 8
ATT_LOOKAHEAD = 3
TM_INPROJ = 256
INPROJ_CHUNK = 256
SSD_CHUNKS_PER_STEP = 2
TS_LRU = 1024
LRU_UNROLL = 4
TM_OUTPROJ = 1024


def _vmem_limit(nbytes):
    return int(min(nbytes * 3 // 2 + (8 << 20), V7X_VMEM_BYTES - (6 << 20)))


def _params(semantics, nbytes):
    return pltpu.CompilerParams(dimension_semantics=semantics, vmem_limit_bytes=_vmem_limit(nbytes))


def _sigmoid(x):
    return 1.0 / (1.0 + jnp.exp(-x))


def _silu(x):
    return x * _sigmoid(x)


def _softplus(x):
    return jnp.maximum(x, 0.0) + jnp.log(1.0 + jnp.exp(-jnp.abs(x)))


def _split3(x):
    hi = x.astype(BF16)
    r1 = x - hi.astype(F32)
    mid = r1.astype(BF16)
    lo = (r1 - mid.astype(F32)).astype(BF16)
    return hi, mid, lo


def _tril_ones(n):
    row = lax.broadcasted_iota(jnp.int32, (n, n), 0)
    col = lax.broadcasted_iota(jnp.int32, (n, n), 1)
    return jnp.where(col <= row, 1.0, 0.0).astype(BF16)


def _cumsum_rows(x, tri):
    hi, mid, lo = _split3(x)
    dot = lambda a: jnp.dot(tri, a, preferred_element_type=F32)
    return dot(hi) + dot(mid) + dot(lo)


def _causal_conv(pad_ref, w_ref, b_ref, cols):
    rows = pad_ref.shape[0] - HALO
    xp = pad_ref[:, cols]
    x = xp[HALO:, :]
    y = b_ref[:, cols] + w_ref[CONV_WIDTH - 1:CONV_WIDTH, cols] * x
    for shift in range(1, CONV_WIDTH):
        kk = CONV_WIDTH - 1 - shift
        y = y + w_ref[kk:kk + 1, cols] * pltpu.roll(xp, shift, axis=0)[HALO:, :]
    pad_ref[0:HALO, cols] = x[rows - HALO:rows, :]
    return y


def _forget_bias_terms(qkvf_ref, fb, cum_ref, bias_ref):
    ts = qkvf_ref.shape[0]
    log_f = -_softplus(-(qkvf_ref[:, 3 * ATT_WIDTH:] + fb))
    c = cum_ref[...] + _cumsum_rows(log_f, _tril_ones(ts))
    cum_ref[...] = c[ts - 1:ts, :]
    for t, term in enumerate(_split3(c * (-LOG2E))):
        bias_ref[t] = term.astype(F32)


def _att_operands_pair(j, qkvf_ref, bias_ref, qg, kg, qt_ref, ka_ref, vt_ref):
    ts = qkvf_ref.shape[0]
    lane = lax.broadcasted_iota(jnp.int32, (ts, V7X_LANES), 1)
    lo_half = lane < ATT_HEAD_DIM
    q_scale = ATT_HEAD_DIM ** -0.5 * LOG2E

    def normed(x, g):
        sq = x * x
        ss_lo = jnp.sum(jnp.where(lo_half, sq, 0.0), axis=1, keepdims=True)
        ss_hi = jnp.sum(jnp.where(lo_half, 0.0, sq), axis=1, keepdims=True)
        inv = jnp.where(lo_half, lax.rsqrt(ss_lo / ATT_HEAD_DIM + EPS),
                        lax.rsqrt(ss_hi / ATT_HEAD_DIM + EPS))
        return x * inv * g

    lo, hi = V7X_LANES * j, V7X_LANES * (j + 1)
    qn = normed(qkvf_ref[:, lo:hi], qg) * q_scale
    kn = normed(qkvf_ref[:, ATT_WIDTH + lo:ATT_WIDTH + hi], kg)
    vp_t = qkvf_ref[:, 2 * ATT_WIDTH + lo:2 * ATT_WIDTH + hi].T
    pad_rows = ATT_VT_ROWS - ATT_HEAD_DIM
    ones_row = jnp.where(lax.broadcasted_iota(jnp.int32, (pad_rows, ts), 0) == 0, 1.0, 0.0)
    vt_ref[0, 2 * j, 0] = jnp.concatenate([vp_t[:ATT_HEAD_DIM, :], ones_row], axis=0).astype(BF16)
    vt_ref[0, 2 * j + 1, 0] = jnp.concatenate([ones_row, vp_t[ATT_HEAD_DIM:, :]], axis=0).astype(BF16)
    qn_t = qn.T
    ones3_rows = jnp.where(lax.broadcasted_iota(jnp.int32, (ATT_HEAD_DIM, ts), 0) < 3, 1.0, 0.0)
    qt_ref[0, 2 * j, 0] = jnp.concatenate([qn_t[:ATT_HEAD_DIM, :], ones3_rows], axis=0).astype(BF16)
    qt_ref[0, 2 * j + 1, 0] = jnp.concatenate([qn_t[ATT_HEAD_DIM:, :], ones3_rows], axis=0).astype(BF16)
    for parity in range(2):
        h = 2 * j + parity
        kh = kn if parity == 0 else pltpu.roll(kn, ATT_HEAD_DIM, axis=1)
        bias = jnp.where(lane == ATT_HEAD_DIM, bias_ref[0, :, h:h + 1],
                         jnp.where(lane == ATT_HEAD_DIM + 1, bias_ref[1, :, h:h + 1],
                                   jnp.where(lane == ATT_HEAD_DIM + 2, bias_ref[2, :, h:h + 1], 0.0)))
        ka_ref[0, h] = jnp.where(lo_half, kh, bias).astype(BF16)


def _inproj_kernel(x_ref, g_ref, w0_ref, w1_ref, w2_ref, w3_ref, qg_ref, kg_ref, fb_ref, scw_ref,
                   scb_ref, lcw_ref, lcb_ref, qt_ref, ka_ref, vt_ref, zatt_ref, zssd_ref, xbc_ref,
                   xlru_ref, zlru_ref, dt_ref, cum_ref, spad_ref, lpad_ref, qkvf_ref, bias_ref):
    tm = x_ref.shape[1]
    w_refs = (w0_ref, w1_ref, w2_ref, w3_ref)

    @pl.when(pl.program_id(1) == 0)
    def _():
        cum_ref[...] = jnp.zeros_like(cum_ref)
        spad_ref[0:HALO, :] = jnp.zeros((HALO, SSD_CONV_DIM), F32)
        lpad_ref[0:HALO, :] = jnp.zeros((HALO, LRU_WIDTH), F32)

    x = x_ref[0]
    ms = jnp.mean(x * x, axis=-1, keepdims=True)
    u = (x * lax.rsqrt(ms + EPS) * g_ref[...]).astype(BF16)

    def proj(name, cols=None):
        run, lo, hi = _SEG[name]
        if cols is not None:
            lo, hi = lo + cols.start, lo + cols.stop
        return jnp.dot(u, w_refs[run][:, lo:hi], preferred_element_type=F32)

    chunk = lambda c: slice(c * INPROJ_CHUNK, (c + 1) * INPROJ_CHUNK)
    plain = [(ref, name, chunk(c))
             for ref, name, width in ((zssd_ref, "z_ssd", SSD_WIDTH), (zatt_ref, "z_att", ATT_WIDTH),
                                      (zlru_ref, "z_lru", LRU_WIDTH))
             for c in range(width // INPROJ_CHUNK)]
    convs = [(spad_ref, scw_ref, scb_ref, xbc_ref, "xbc", chunk(c), True)
             for c in range(SSD_CONV_DIM // INPROJ_CHUNK)]
    convs += [(lpad_ref, lcw_ref, lcb_ref, xlru_ref, "x_lru", chunk(c), False)
              for c in range(LRU_WIDTH // INPROJ_CHUNK)]

    def matmul_plain():
        if plain:
            ref, name, cols = plain.pop(0)
            ref[0, :, cols] = proj(name, cols)

    def matmul_conv(ci):
        if ci < len(convs):
            pad_ref, _, _, _, name, cols, _ = convs[ci]
            pad_ref[HALO:HALO + tm, cols] = proj(name, cols)

    def finish_conv(ci):
        pad_ref, cw_ref, cb_ref, out_ref, _, cols, silu = convs[ci]
        y = _causal_conv(pad_ref, cw_ref, cb_ref, cols)
        out_ref[0, :, cols] = _silu(y) if silu else y

    for si, name in enumerate(("q", "k", "v")):
        qkvf_ref[:, si * ATT_WIDTH:(si + 1) * ATT_WIDTH] = proj(name)
    fdt = proj("fdt")
    qkvf_ref[:, 3 * ATT_WIDTH:] = fdt
    dt_ref[0] = fdt
    matmul_conv(0)
    matmul_plain()
    _forget_bias_terms(qkvf_ref, fb_ref[...], cum_ref, bias_ref)
    pairs = list(range(ATT_HEADS // 2))
    for ci in range(len(convs)):
        matmul_conv(ci + 1)
        if pairs:
            _att_operands_pair(pairs.pop(0), qkvf_ref, bias_ref, qg_ref[...], kg_ref[...],
                               qt_ref, ka_ref, vt_ref)
        matmul_plain()
        finish_conv(ci)
    assert not pairs and not plain


def _inproj(h, g, w_runs, qg, kg, fb, ssd_cw, ssd_cb, lru_cw, lru_cb):
    b, s, _ = h.shape
    tm = TM_INPROJ
    tok = lambda wd: pl.BlockSpec((1, tm, wd), lambda i, t: (i, t, 0))
    par = lambda r, wd: pl.BlockSpec((r, wd), lambda i, t: (0, 0))
    head = pl.BlockSpec((1, ATT_HEADS, tm, V7X_LANES), lambda i, t: (i, 0, t, 0))
    head_t = pl.BlockSpec((1, ATT_HEADS, 1, ATT_VT_ROWS, tm), lambda i, t: (i, 0, t, 0, 0))
    head_qt = pl.BlockSpec((1, ATT_HEADS, 1, V7X_LANES, tm), lambda i, t: (i, 0, t, 0, 0))
    tok_widths = (ATT_WIDTH, SSD_WIDTH, SSD_CONV_DIM, LRU_WIDTH, LRU_WIDTH, V7X_LANES)
    nbytes = 2 * (tm * D_MODEL * 4 + D_MODEL * D_IN_PAD * 2 + tm * sum(tok_widths) * 4
                  + 3 * ATT_HEADS * tm * V7X_LANES * 2) \
        + (HALO + tm) * (SSD_CONV_DIM + LRU_WIDTH) * 4 + 4 * tm * SSD_CONV_DIM * 4
    return pl.pallas_call(
        _inproj_kernel,
        grid=(b, s // tm),
        in_specs=[tok(D_MODEL), par(1, D_MODEL)] + [par(D_MODEL, wd) for wd in _RUN_WIDTHS]
        + [par(1, V7X_LANES), par(1, V7X_LANES), par(1, V7X_LANES),
                  par(CONV_WIDTH, SSD_CONV_DIM), par(1, SSD_CONV_DIM),
                  par(CONV_WIDTH, LRU_WIDTH), par(1, LRU_WIDTH)],
        out_specs=[head_qt, head, head_t] + [tok(wd) for wd in tok_widths],
        out_shape=[jax.ShapeDtypeStruct((b, ATT_HEADS, s // tm, V7X_LANES, tm), BF16),
                   jax.ShapeDtypeStruct((b, ATT_HEADS, s, V7X_LANES), BF16),
                   jax.ShapeDtypeStruct((b, ATT_HEADS, s // tm, ATT_VT_ROWS, tm), BF16)]
        + [jax.ShapeDtypeStruct((b, s, wd), F32) for wd in tok_widths],
        scratch_shapes=[pltpu.VMEM((1, V7X_LANES), F32),
                        pltpu.VMEM((HALO + tm, SSD_CONV_DIM), F32),
                        pltpu.VMEM((HALO + tm, LRU_WIDTH), F32),
                        pltpu.VMEM((tm, 3 * ATT_WIDTH + V7X_LANES), F32),
                        pltpu.VMEM((3, tm, V7X_LANES), F32)],
        compiler_params=_params(("parallel", "arbitrary"), nbytes),
        name="inproj",
    )(h, g, *w_runs, qg, kg, fb, ssd_cw, ssd_cb, lru_cw, lru_cb)


def _att_kernel(qt_ref, ka_ref, vt_ref, z_ref, o_ref, s_ref, cmax_ref, m_ref, acc_ref,
                *, tq, tk, tv, heads):
    qi = pl.program_id(2)
    subs = tq // tk
    tqb = qt_ref.shape[-1]
    assert tk % tqb == 0
    chains = [(sub, hh) for sub in range(subs) for hh in range(heads)]
    slots = ATT_LOOKAHEAD + 1

    m_ref[...] = jnp.full(m_ref.shape, -jnp.inf, F32)
    acc_ref[...] = jnp.zeros(acc_ref.shape, F32)

    assert len(chains) % slots == 0

    def causal_mask(s_t, sub, q0):
        key_r = lax.broadcasted_iota(jnp.int32, s_t.shape, 0)
        qry_c = lax.broadcasted_iota(jnp.int32, s_t.shape, 1)
        return jnp.where(key_r + (sub * tk - q0) <= qry_c, s_t, -jnp.inf)

    def issue_logits(j, idx, q0=0, diagonal=False):
        sub, hh = chains[idx]
        k0 = pl.multiple_of((j * subs + sub) * tk, tk)
        keys = ka_ref[0, hh, pl.ds(k0, tk), :]
        for qb in range(q0 // tqb, tq // tqb):
            cols = slice(qb * tqb, (qb + 1) * tqb)
            s_t = jnp.dot(keys, qt_ref[0, hh, qb], preferred_element_type=F32)
            if diagonal:
                s_t = causal_mask(s_t, sub, qb * tqb)
            s_ref[idx % slots, :, cols] = s_t
            cmax_ref[idx % slots, :, cols] = jnp.max(s_t, axis=0, keepdims=True)

    def step(j, diagonal):
        for idx, (sub, hh) in enumerate(chains):
            q0 = sub * tk if diagonal else 0
            s_t = s_ref[idx % slots, :, q0:]
            cmax = cmax_ref[idx % slots, :, q0:]
            ahead = idx + ATT_LOOKAHEAD
            if ahead < len(chains):
                issue_logits(j, ahead, chains[ahead][0] * tk if diagonal else 0, diagonal)
            elif not diagonal:
                issue_logits(j + 1, ahead - len(chains))
            if diagonal and idx < ATT_LOOKAHEAD:
                s_t = causal_mask(s_t, sub, q0)
                cmax = jnp.max(s_t, axis=0, keepdims=True)
            m = m_ref[hh, :, q0:]
            m_new = jnp.maximum(m, cmax)
            m_ref[hh, :, q0:] = m_new
            p_t = jnp.exp2(s_t - m_new).astype(BF16)
            acc = acc_ref[hh, :, q0:] * jnp.exp2(m - m_new)
            for vb in range(tk // tv):
                acc = acc + jnp.dot(vt_ref[0, hh, (j * subs + sub) * (tk // tv) + vb],
                                    p_t[vb * tv:(vb + 1) * tv, :], preferred_element_type=F32)
            acc_ref[hh, :, q0:] = acc

    for idx in range(ATT_LOOKAHEAD):
        issue_logits(0, idx)

    @pl.loop(0, qi)
    def _(j):
        step(j, False)

    step(qi, True)

    lead = ATT_VT_ROWS - ATT_HEAD_DIM
    for pr in range(heads // 2):
        even, odd = acc_ref[2 * pr], acc_ref[2 * pr + 1]
        o_t = jnp.concatenate(
            [even[0:ATT_HEAD_DIM, :] * (1.0 / even[ATT_HEAD_DIM:ATT_HEAD_DIM + 1, :]),
             odd[lead:, :] * (1.0 / odd[0:1, :])], axis=0)
        lanes = slice(V7X_LANES * pr, V7X_LANES * (pr + 1))
        o_ref[0, :, lanes] = (o_t.T * _silu(z_ref[0, :, lanes])).astype(o_ref.dtype)


def _attention(qt, ka, vt, z_att):
    b, _, s, _ = ka.shape
    tq, tk, hp = TQ_ATT, TK_ATT, ATT_HEADS_PER_STEP
    assert tq % tk == 0 and hp % 2 == 0 and ATT_HEADS % hp == 0
    wd = hp * ATT_HEAD_DIM
    tqb = qt.shape[-1]
    q_tile = pl.BlockSpec((1, hp, tq // tqb, V7X_LANES, tqb), lambda i, j, t: (i, j, t, 0, 0))
    k_full = pl.BlockSpec((1, hp, s, V7X_LANES), lambda i, j, t: (i, j, 0, 0))
    tv = vt.shape[-1]
    assert tk % tv == 0
    v_full = pl.BlockSpec((1, hp, s // tv, ATT_VT_ROWS, tv), lambda i, j, t: (i, j, 0, 0, 0))
    tok = pl.BlockSpec((1, tq, wd), lambda i, j, t: (i, t, j))
    nbytes = 2 * (hp * tq * V7X_LANES * 2 + 2 * hp * s * V7X_LANES * 2 + tq * wd * 6) \
        + (2 * ATT_LOOKAHEAD + 4) * tq * tk * 4
    return pl.pallas_call(
        functools.partial(_att_kernel, tq=tq, tk=tk, tv=tv, heads=hp),
        grid=(b, ATT_HEADS // hp, s // tq),
        in_specs=[q_tile, k_full, v_full, tok],
        out_specs=tok,
        out_shape=jax.ShapeDtypeStruct((b, s, ATT_WIDTH), BF16),
        scratch_shapes=[pltpu.VMEM((ATT_LOOKAHEAD + 1, tk, tq), F32),
                        pltpu.VMEM((ATT_LOOKAHEAD + 1, 1, tq), F32),
                        pltpu.VMEM((hp, 1, tq), F32),
                        pltpu.VMEM((hp, ATT_VT_ROWS, tq), F32)],
        compiler_params=_params(("parallel", "parallel", "arbitrary"), nbytes),
        name="att",
    )(qt, ka, vt, z_att)


def _ssd_kernel(xbc_ref, z_ref, dt_ref, dtb_ref, alog_ref, dsk_ref, ng_ref, y_ref, state_ref,
                *, chunks):
    L = SSD_CHUNK
    n = SSD_STATE
    gw = SSD_WIDTH // SSD_GROUPS

    @pl.when(pl.program_id(1) == 0)
    def _():
        state_ref[...] = jnp.zeros_like(state_ref)

    row = lax.broadcasted_iota(jnp.int32, (L, L), 0)
    col = lax.broadcasted_iota(jnp.int32, (L, L), 1)
    causal = col <= row
    tri = jnp.where(causal, 1.0, 0.0).astype(BF16)
    lo_half = lax.broadcasted_iota(jnp.int32, (L, V7X_LANES), 1) < SSD_HEAD_DIM
    a_neg = -jnp.exp(alog_ref[...])

    def decay_terms(ci):
        rows = slice(ci * L, (ci + 1) * L)
        dt = _softplus(dt_ref[0, rows, :] + dtb_ref[...])
        acs = _cumsum_rows(dt * a_neg, tri)
        tot = acs[L - 1:L, :]
        return (acs, jnp.exp(acs), jnp.exp(tot), acs.T, dt.T, (dt * jnp.exp(tot - acs)).T)

    terms = decay_terms(0)
    for ci in range(chunks):
        rows = slice(ci * L, (ci + 1) * L)
        acs, e_acs, e_tot, acs_t, dt_t, w_t = terms
        if ci + 1 < chunks:
            terms = decay_terms(ci + 1)

        y_tiles = []
        for g in range(SSD_GROUPS):
            b_lo = SSD_WIDTH + n * g
            c_lo = SSD_WIDTH + SSD_GROUPS * n + n * g
            bg_t = xbc_ref[0, rows, b_lo:b_lo + n].T
            cg = xbc_ref[0, rows, c_lo:c_lo + n].astype(BF16)
            cb = jnp.dot(cg, bg_t.astype(BF16), preferred_element_type=F32)
            s_prev = state_ref[g]
            cs = jnp.dot(cg, s_prev.astype(BF16), preferred_element_type=F32)
            for jj in range(gw // V7X_LANES):
                j = g * (gw // V7X_LANES) + jj
                he, ho = SSD_DT_LANE0 + 2 * j, SSD_DT_LANE0 + 2 * j + 1
                xp = xbc_ref[0, rows, V7X_LANES * j:V7X_LANES * (j + 1)]
                rhs = jnp.concatenate([jnp.where(lo_half, xp, 0.0), jnp.where(lo_half, 0.0, xp)],
                                      axis=0).astype(BF16)

                def scores(h):
                    seg = acs[:, h:h + 1] - acs_t[h:h + 1, :]
                    return cb * jnp.exp(jnp.where(causal, seg, -jnp.inf)) * dt_t[h:h + 1, :]

                lhs = jnp.concatenate([scores(he), scores(ho)], axis=1).astype(BF16)
                y_diag = jnp.dot(lhs, rhs, preferred_element_type=F32)
                e_pair = jnp.where(lo_half, e_acs[:, he:he + 1], e_acs[:, ho:ho + 1])
                y_off = cs[:, V7X_LANES * jj:V7X_LANES * (jj + 1)] * e_pair
                y_tiles.append(y_diag + y_off + xp * dsk_ref[:, V7X_LANES * j:V7X_LANES * (j + 1)])

                lhs_s = jnp.concatenate([bg_t * w_t[he:he + 1, :], bg_t * w_t[ho:ho + 1, :]],
                                        axis=1).astype(BF16)
                s_new = jnp.dot(lhs_s, rhs, preferred_element_type=F32)
                dec = jnp.where(lo_half[0:1], e_tot[:, he:he + 1], e_tot[:, ho:ho + 1])
                state_ref[g, :, V7X_LANES * jj:V7X_LANES * (jj + 1)] = (
                    s_prev[:, V7X_LANES * jj:V7X_LANES * (jj + 1)] * dec + s_new)

        y = jnp.concatenate(y_tiles, axis=1)
        gated = y * _silu(z_ref[0, rows, :])
        outs = []
        for g in range(SSD_GROUPS):
            gg = gated[:, gw * g:gw * (g + 1)]
            ms = jnp.mean(gg * gg, axis=-1, keepdims=True)
            outs.append(gg * lax.rsqrt(ms + EPS))
        y_ref[0, rows, :] = (jnp.concatenate(outs, axis=1) * ng_ref[...]).astype(y_ref.dtype)


def _ssd(xbc, z, dt_raw, dt_bias, a_log, d_skip, norm_g):
    b, s, _ = xbc.shape
    chunks = SSD_CHUNKS_PER_STEP
    ts = chunks * SSD_CHUNK
    tok = lambda wd: pl.BlockSpec((1, ts, wd), lambda i, t: (i, t, 0))
    par = lambda r, wd: pl.BlockSpec((r, wd), lambda i, t: (0, 0))
    nbytes = 2 * ts * (SSD_CONV_DIM + SSD_WIDTH + V7X_LANES) * 4 + 2 * ts * SSD_WIDTH * 2 \
        + SSD_GROUPS * SSD_STATE * SSD_WIDTH * 2 + 8 * SSD_CHUNK * SSD_CONV_DIM * 4
    return pl.pallas_call(
        functools.partial(_ssd_kernel, chunks=chunks),
        grid=(b, s // ts),
        in_specs=[tok(SSD_CONV_DIM), tok(SSD_WIDTH), tok(V7X_LANES),
                  par(1, V7X_LANES), par(1, V7X_LANES), par(1, SSD_WIDTH), par(1, SSD_WIDTH)],
        out_specs=tok(SSD_WIDTH),
        out_shape=jax.ShapeDtypeStruct((b, s, SSD_WIDTH), BF16),
        scratch_shapes=[pltpu.VMEM((SSD_GROUPS, SSD_STATE, SSD_WIDTH // SSD_GROUPS), F32)],
        compiler_params=_params(("parallel", "arbitrary"), nbytes),
        name="ssd",
    )(xbc, z, dt_raw, dt_bias, a_log, d_skip, norm_g)


def _lru_kernel(x_ref, z_ref, wg_ref, bg_ref, lam_ref, y_ref, a_ref, b_ref, h_ref):
    ts = x_ref.shape[1]
    groups = ts // V7X_SUBLANES
    first_tile = pl.program_id(1) == 0

    @pl.when(first_tile)
    def _():
        h_ref[...] = jnp.zeros_like(h_ref)

    xc = x_ref[0]
    gates = jnp.dot(xc.astype(BF16), wg_ref[...], preferred_element_type=F32) + bg_ref[...]
    r = _sigmoid(gates[:, :LRU_WIDTH])
    i = _sigmoid(gates[:, LRU_WIDTH:])
    a = jnp.exp2(r * ((-LRU_C * LOG2E) * _softplus(-lam_ref[...])))
    gap = 1.0 - a * a
    mult = jnp.where(gap > 0.0, gap * lax.rsqrt(gap), 0.0)
    seq_start = first_tile & (lax.broadcasted_iota(jnp.int32, (ts, LRU_WIDTH), 0) == 0)
    mult = jnp.where(seq_start, 1.0, mult)
    b = mult * (i * xc)

    aa = a.reshape(groups, V7X_SUBLANES, LRU_WIDTH)
    bb = b.reshape(groups, V7X_SUBLANES, LRU_WIDTH)
    sub = lax.broadcasted_iota(jnp.int32, (groups, V7X_SUBLANES, LRU_WIDTH), 1)
    for d in (1, 2, 4):
        keep = sub >= d
        bb = jnp.where(keep, aa * pltpu.roll(bb, d, axis=1) + bb, bb)
        aa = jnp.where(keep, aa * pltpu.roll(aa, d, axis=1), aa)
    a_ref[...] = aa.reshape(ts, LRU_WIDTH)
    b_ref[...] = bb.reshape(ts, LRU_WIDTH)

    def group(gi, h):
        r0 = pl.multiple_of(gi * V7X_SUBLANES, V7X_SUBLANES)
        rows = pl.ds(r0, V7X_SUBLANES)
        hh = a_ref[rows, :] * h + b_ref[rows, :]
        b_ref[rows, :] = hh
        return hh[V7X_SUBLANES - 1:V7X_SUBLANES, :]

    h_ref[...] = lax.fori_loop(0, groups, group, h_ref[...], unroll=LRU_UNROLL)
    y_ref[0] = (b_ref[...] * _silu(z_ref[0])).astype(y_ref.dtype)


def _lru(x_lru, z_lru, w_gate, b_gate, lam):
    b, s, _ = x_lru.shape
    ts = TS_LRU
    tok = pl.BlockSpec((1, ts, LRU_WIDTH), lambda i, t: (i, t, 0))
    par = lambda r, wd: pl.BlockSpec((r, wd), lambda i, t: (0, 0))
    nbytes = 2 * 2 * ts * LRU_WIDTH * 4 + 2 * ts * LRU_WIDTH * 2 + 2 * LRU_WIDTH * 2 * LRU_WIDTH * 2 \
        + 2 * ts * LRU_WIDTH * 4 + 8 * ts * LRU_WIDTH * 4
    return pl.pallas_call(
        _lru_kernel,
        grid=(b, s // ts),
        in_specs=[tok, tok, par(LRU_WIDTH, 2 * LRU_WIDTH), par(1, 2 * LRU_WIDTH), par(1, LRU_WIDTH)],
        out_specs=tok,
        out_shape=jax.ShapeDtypeStruct((b, s, LRU_WIDTH), BF16),
        scratch_shapes=[pltpu.VMEM((ts, LRU_WIDTH), F32),
                        pltpu.VMEM((ts, LRU_WIDTH), F32),
                        pltpu.VMEM((1, LRU_WIDTH), F32)],
        compiler_params=_params(("parallel", "arbitrary"), nbytes),
        name="lru",
    )(x_lru, z_lru, w_gate, b_gate, lam)


def _outproj_kernel(h_ref, ya_ref, ys_ref, yl_ref, w_ref, o_ref):
    a0, a1, a2 = ATT_WIDTH, ATT_WIDTH + SSD_WIDTH, MIX_WIDTH
    o_ref[...] = (h_ref[...]
                  + jnp.dot(ya_ref[...], w_ref[0:a0, :], preferred_element_type=F32)
                  + jnp.dot(ys_ref[...], w_ref[a0:a1, :], preferred_element_type=F32)
                  + jnp.dot(yl_ref[...], w_ref[a1:a2, :], preferred_element_type=F32))


def _outproj(h, y_att, y_ssd, y_lru, w):
    t = h.shape[0]
    tm = TM_OUTPROJ
    tok = lambda wd: pl.BlockSpec((tm, wd), lambda i: (i, 0))
    nbytes = 2 * (2 * tm * D_MODEL * 4 + tm * MIX_WIDTH * 2 + MIX_WIDTH * D_MODEL * 2)
    return pl.pallas_call(
        _outproj_kernel,
        grid=(t // tm,),
        in_specs=[tok(D_MODEL), tok(ATT_WIDTH), tok(SSD_WIDTH), tok(LRU_WIDTH),
                  pl.BlockSpec((MIX_WIDTH, D_MODEL), lambda i: (0, 0))],
        out_specs=tok(D_MODEL),
        out_shape=jax.ShapeDtypeStruct((t, D_MODEL), F32),
        compiler_params=_params(("parallel",), nbytes),
        name="outproj",
    )(h, y_att, y_ssd, y_lru, w)


def _pad_lanes(v, lane0=0, width=V7X_LANES):
    return jnp.pad(v, (lane0, width - lane0 - v.shape[0]))[None, :]


def _regroup_w_in(w):
    off = {name: sum(IN_SIZES[:i]) for i, name in enumerate(IN_NAMES)}
    end = sum(IN_SIZES)
    assert [n for run in _RUNS[:-1] for n, _ in run] == [n for n in IN_NAMES
                                                          if n not in ("f_raw", "dt_raw")]
    narrow = jnp.concatenate([w[:, off["f_raw"]:off["z_ssd"]], w[:, off["dt_raw"]:off["x_lru"]]], axis=1)
    runs = (w[:, :off["f_raw"]], w[:, off["z_ssd"]:off["dt_raw"]], w[:, off["x_lru"]:end],
            jnp.pad(narrow, ((0, 0), (0, V7X_LANES - narrow.shape[1]))))
    return tuple(r.astype(BF16) for r in runs)


def _block_diag(w):
    nb, d, e = w.shape
    tiled = jnp.tile(w.reshape(nb * d, e), (1, nb))
    row = lax.broadcasted_iota(jnp.int32, tiled.shape, 0) // d
    col = lax.broadcasted_iota(jnp.int32, tiled.shape, 1) // e
    return jnp.where(row == col, tiled, 0.0)


def kernel(x, norm_g, w_in, q_norm_g, k_norm_g, forget_b, ssd_conv_w, ssd_conv_b, ssd_dt_bias,
           ssd_a_log, ssd_d, ssd_norm_g, lru_conv_w, lru_conv_b, lru_w_a, lru_b_a, lru_w_x, lru_b_x,
           lru_lambda, w_out):
    bsz, seq, d = x.shape
    assert d == D_MODEL and seq % max(TQ_ATT, TM_INPROJ, TS_LRU, SSD_CHUNKS_PER_STEP * SSD_CHUNK) == 0
    assert (bsz * seq) % TM_OUTPROJ == 0
    depth = w_in.shape[0]
    h = x.astype(F32).reshape(bsz * seq, D_MODEL)
    for l in range(depth):
        qt, ka, vt, z_att, z_ssd, xbc, x_lru, z_lru, dt_raw = _inproj(
            h.reshape(bsz, seq, D_MODEL), norm_g[l][None, :], _regroup_w_in(w_in[l]),
            jnp.tile(q_norm_g[l], 2)[None, :], jnp.tile(k_norm_g[l], 2)[None, :],
            _pad_lanes(forget_b[l]), ssd_conv_w[l], ssd_conv_b[l][None, :],
            lru_conv_w[l], lru_conv_b[l][None, :])
        y_att = _attention(qt, ka, vt, z_att)
        y_ssd = _ssd(xbc, z_ssd, dt_raw, _pad_lanes(ssd_dt_bias[l], SSD_DT_LANE0),
                     _pad_lanes(ssd_a_log[l], SSD_DT_LANE0),
                     jnp.repeat(ssd_d[l], SSD_HEAD_DIM)[None, :], ssd_norm_g[l][None, :])
        b_gate = jnp.concatenate([lru_b_a[l], lru_b_x[l]])[None, :]
        w_gate = jnp.concatenate([_block_diag(lru_w_a[l]), _block_diag(lru_w_x[l])], axis=1).astype(BF16)
        y_lru = _lru(x_lru, z_lru, w_gate, b_gate, lru_lambda[l][None, :])
        h = _outproj(h, y_att.reshape(bsz * seq, ATT_WIDTH), y_ssd.reshape(bsz * seq, SSD_WIDTH),
                     y_lru.reshape(bsz * seq, LRU_WIDTH), w_out[l].astype(BF16))
    return h.reshape(bsz, seq, D_MODEL).astype(x.dtype)
```

```python
import functools

import jax
import jax.numpy as jnp
from jax import lax
from jax.experimental import pallas as pl
from jax.experimental.pallas import tpu as pltpu

F32 = jnp.float32
BF16 = jnp.bfloat16

V7X_LANES = 128
V7X_SUBLANES = 8
V7X_VMEM_BYTES = 64 * 1024 * 1024

D_MODEL = 1024
EPS = 1e-6
CONV_WIDTH = 4
MIX_WIDTH = 2 * D_MODEL
ATT_HEAD_DIM = 64
ATT_WIDTH = MIX_WIDTH // 4
ATT_HEADS = ATT_WIDTH // ATT_HEAD_DIM
SSD_HEAD_DIM = 64
SSD_WIDTH = MIX_WIDTH // 2
SSD_HEADS = SSD_WIDTH // SSD_HEAD_DIM
SSD_GROUPS = 2
SSD_STATE = 128
SSD_CHUNK = 128
SSD_CONV_DIM = SSD_WIDTH + 2 * SSD_GROUPS * SSD_STATE
LRU_WIDTH = MIX_WIDTH // 4
LRU_BLOCKS = 8
LRU_C = 8.0
IN_SIZES = (ATT_WIDTH, ATT_WIDTH, ATT_WIDTH, ATT_WIDTH, ATT_HEADS,
            SSD_WIDTH, SSD_CONV_DIM, SSD_HEADS, LRU_WIDTH, LRU_WIDTH)
IN_NAMES = ("q", "k", "v", "z_att", "f_raw", "z_ssd", "xbc", "dt_raw", "x_lru", "z_lru")

LOG2E = 1.4426950408889634
HALO = V7X_SUBLANES

SSD_DT_LANE0 = ATT_HEADS
_RUNS = ((("q", ATT_WIDTH), ("k", ATT_WIDTH), ("v", ATT_WIDTH), ("z_att", ATT_WIDTH)),
         (("z_ssd", SSD_WIDTH), ("xbc", SSD_CONV_DIM)),
         (("x_lru", LRU_WIDTH), ("z_lru", LRU_WIDTH)),
         (("fdt", V7X_LANES),))
_SEGS = tuple(seg for run in _RUNS for seg in run)
_SEG = {name: (sum(w for _, w in _SEGS[:si]), sum(w for _, w in _SEGS[:si + 1]))
        for si, (name, _) in enumerate(_SEGS)}
D_IN_PAD = sum(w for _, w in _SEGS)

TQ_ATT = 1024
TK_ATT = 256
ATT_VT_ROWS = 80
ATT_HEADS_PER_STEP = 8
ATT_LOOKAHEAD = 3
TM_INPROJ = 256
INPROJ_CHUNK = 256
SSD_CHUNKS_PER_STEP = 2
TS_LRU = 1024
LRU_UNROLL = 4
TM_OUTPROJ = 1024
TR_REGROUP = 128


def _vmem_limit(nbytes):
    return int(min(nbytes * 3 // 2 + (8 << 20), V7X_VMEM_BYTES - (6 << 20)))


def _params(semantics, nbytes):
    return pltpu.CompilerParams(dimension_semantics=semantics, vmem_limit_bytes=_vmem_limit(nbytes))


def _sigmoid(x):
    return 1.0 / (1.0 + jnp.exp(-x))


def _silu(x):
    return x * _sigmoid(x)


def _softplus(x):
    return jnp.maximum(x, 0.0) + jnp.log(1.0 + jnp.exp(-jnp.abs(x)))


def _split3(x):
    hi = x.astype(BF16)
    r1 = x - hi.astype(F32)
    mid = r1.astype(BF16)
    lo = (r1 - mid.astype(F32)).astype(BF16)
    return hi, mid, lo


def _tril_ones(n):
    row = lax.broadcasted_iota(jnp.int32, (n, n), 0)
    col = lax.broadcasted_iota(jnp.int32, (n, n), 1)
    return jnp.where(col <= row, 1.0, 0.0).astype(BF16)


def _cumsum_rows(x, tri):
    hi, mid, lo = _split3(x)
    dot = lambda a: jnp.dot(tri, a, preferred_element_type=F32)
    return dot(hi) + dot(mid) + dot(lo)


def _causal_conv(pad_ref, w_ref, b_ref, cols):
    rows = pad_ref.shape[0] - HALO
    xp = pad_ref[:, cols]
    x = xp[HALO:, :]
    y = b_ref[:, cols] + w_ref[CONV_WIDTH - 1:CONV_WIDTH, cols] * x
    for shift in range(1, CONV_WIDTH):
        kk = CONV_WIDTH - 1 - shift
        y = y + w_ref[kk:kk + 1, cols] * pltpu.roll(xp, shift, axis=0)[HALO:, :]
    pad_ref[0:HALO, cols] = x[rows - HALO:rows, :]
    return y


def _forget_bias_terms(qkvf_ref, fb, cum_ref, bias_ref):
    ts = qkvf_ref.shape[0]
    log_f = -_softplus(-(qkvf_ref[:, 3 * ATT_WIDTH:] + fb))
    c = cum_ref[...] + _cumsum_rows(log_f, _tril_ones(ts))
    cum_ref[...] = c[ts - 1:ts, :]
    for t, term in enumerate(_split3(c * (-LOG2E))):
        bias_ref[t] = term.astype(F32)


def _att_operands_pair(j, qkvf_ref, bias_ref, qg, kg, qt_ref, ka_ref, vt_ref):
    ts = qkvf_ref.shape[0]
    lane = lax.broadcasted_iota(jnp.int32, (ts, V7X_LANES), 1)
    lo_half = lane < ATT_HEAD_DIM
    q_scale = ATT_HEAD_DIM ** -0.5 * LOG2E

    def normed(x, g):
        sq = x * x
        ss_lo = jnp.sum(jnp.where(lo_half, sq, 0.0), axis=1, keepdims=True)
        ss_hi = jnp.sum(jnp.where(lo_half, 0.0, sq), axis=1, keepdims=True)
        inv = jnp.where(lo_half, lax.rsqrt(ss_lo / ATT_HEAD_DIM + EPS),
                        lax.rsqrt(ss_hi / ATT_HEAD_DIM + EPS))
        return x * inv * g

    lo, hi = V7X_LANES * j, V7X_LANES * (j + 1)
    qn = normed(qkvf_ref[:, lo:hi], qg) * q_scale
    kn = normed(qkvf_ref[:, ATT_WIDTH + lo:ATT_WIDTH + hi], kg)
    vp_t = qkvf_ref[:, 2 * ATT_WIDTH + lo:2 * ATT_WIDTH + hi].T
    pad_rows = ATT_VT_ROWS - ATT_HEAD_DIM
    ones_row = jnp.where(lax.broadcasted_iota(jnp.int32, (pad_rows, ts), 0) == 0, 1.0, 0.0)
    vt_ref[0, 2 * j, 0] = jnp.concatenate([vp_t[:ATT_HEAD_DIM, :], ones_row], axis=0).astype(BF16)
    vt_ref[0, 2 * j + 1, 0] = jnp.concatenate([ones_row, vp_t[ATT_HEAD_DIM:, :]], axis=0).astype(BF16)
    qn_t = qn.T
    ones3_rows = jnp.where(lax.broadcasted_iota(jnp.int32, (ATT_HEAD_DIM, ts), 0) < 3, 1.0, 0.0)
    qt_ref[0, 2 * j, 0] = jnp.concatenate([qn_t[:ATT_HEAD_DIM, :], ones3_rows], axis=0).astype(BF16)
    qt_ref[0, 2 * j + 1, 0] = jnp.concatenate([qn_t[ATT_HEAD_DIM:, :], ones3_rows], axis=0).astype(BF16)
    for parity in range(2):
        h = 2 * j + parity
        kh = kn if parity == 0 else pltpu.roll(kn, ATT_HEAD_DIM, axis=1)
        bias = jnp.where(lane == ATT_HEAD_DIM, bias_ref[0, :, h:h + 1],
                         jnp.where(lane == ATT_HEAD_DIM + 1, bias_ref[1, :, h:h + 1],
                                   jnp.where(lane == ATT_HEAD_DIM + 2, bias_ref[2, :, h:h + 1], 0.0)))
        ka_ref[0, h] = jnp.where(lo_half, kh, bias).astype(BF16)


def _inproj_kernel(x_ref, g_ref, w_ref, qg_ref, kg_ref, fb_ref, scw_ref, scb_ref, lcw_ref, lcb_ref,
                   qt_ref, ka_ref, vt_ref, zatt_ref, zssd_ref, xbc_ref, xlru_ref, zlru_ref, dt_ref,
                   cum_ref, spad_ref, lpad_ref, qkvf_ref, bias_ref):
    tm = x_ref.shape[1]

    @pl.when(pl.program_id(1) == 0)
    def _():
        cum_ref[...] = jnp.zeros_like(cum_ref)
        spad_ref[0:HALO, :] = jnp.zeros((HALO, SSD_CONV_DIM), F32)
        lpad_ref[0:HALO, :] = jnp.zeros((HALO, LRU_WIDTH), F32)

    x = x_ref[0]
    ms = jnp.mean(x * x, axis=-1, keepdims=True)
    u = (x * lax.rsqrt(ms + EPS) * g_ref[...]).astype(BF16)

    def proj(name, cols=None):
        lo, hi = _SEG[name]
        if cols is not None:
            lo, hi = lo + cols.start, lo + cols.stop
        return jnp.dot(u, w_ref[:, lo:hi], preferred_element_type=F32)

    chunk = lambda c: slice(c * INPROJ_CHUNK, (c + 1) * INPROJ_CHUNK)
    plain = [(ref, name, chunk(c))
             for ref, name, width in ((zssd_ref, "z_ssd", SSD_WIDTH), (zatt_ref, "z_att", ATT_WIDTH),
                                      (zlru_ref, "z_lru", LRU_WIDTH))
             for c in range(width // INPROJ_CHUNK)]
    convs = [(spad_ref, scw_ref, scb_ref, xbc_ref, "xbc", chunk(c), True)
             for c in range(SSD_CONV_DIM // INPROJ_CHUNK)]
    convs += [(lpad_ref, lcw_ref, lcb_ref, xlru_ref, "x_lru", chunk(c), False)
              for c in range(LRU_WIDTH // INPROJ_CHUNK)]

    def matmul_plain():
        if plain:
            ref, name, cols = plain.pop(0)
            ref[0, :, cols] = proj(name, cols)

    def matmul_conv(ci):
        if ci < len(convs):
            pad_ref, _, _, _, name, cols, _ = convs[ci]
            pad_ref[HALO:HALO + tm, cols] = proj(name, cols)

    def finish_conv(ci):
        pad_ref, cw_ref, cb_ref, out_ref, _, cols, silu = convs[ci]
        y = _causal_conv(pad_ref, cw_ref, cb_ref, cols)
        out_ref[0, :, cols] = _silu(y) if silu else y

    for si, name in enumerate(("q", "k", "v")):
        qkvf_ref[:, si * ATT_WIDTH:(si + 1) * ATT_WIDTH] = proj(name)
    fdt = proj("fdt")
    qkvf_ref[:, 3 * ATT_WIDTH:] = fdt
    dt_ref[0] = fdt
    matmul_conv(0)
    matmul_plain()
    _forget_bias_terms(qkvf_ref, fb_ref[...], cum_ref, bias_ref)
    pairs = list(range(ATT_HEADS // 2))
    for ci in range(len(convs)):
        matmul_conv(ci + 1)
        if pairs:
            _att_operands_pair(pairs.pop(0), qkvf_ref, bias_ref, qg_ref[...], kg_ref[...],
                               qt_ref, ka_ref, vt_ref)
        matmul_plain()
        finish_conv(ci)
    assert not pairs and not plain


def _inproj(h, g, w, qg, kg, fb, ssd_cw, ssd_cb, lru_cw, lru_cb):
    b, s, _ = h.shape
    tm = TM_INPROJ
    tok = lambda wd: pl.BlockSpec((1, tm, wd), lambda i, t: (i, t, 0))
    par = lambda r, wd: pl.BlockSpec((r, wd), lambda i, t: (0, 0))
    head = pl.BlockSpec((1, ATT_HEADS, tm, V7X_LANES), lambda i, t: (i, 0, t, 0))
    head_t = pl.BlockSpec((1, ATT_HEADS, 1, ATT_VT_ROWS, tm), lambda i, t: (i, 0, t, 0, 0))
    head_qt = pl.BlockSpec((1, ATT_HEADS, 1, V7X_LANES, tm), lambda i, t: (i, 0, t, 0, 0))
    tok_widths = (ATT_WIDTH, SSD_WIDTH, SSD_CONV_DIM, LRU_WIDTH, LRU_WIDTH, V7X_LANES)
    nbytes = 2 * (tm * D_MODEL * 4 + D_MODEL * D_IN_PAD * 2 + tm * sum(tok_widths) * 4
                  + 3 * ATT_HEADS * tm * V7X_LANES * 2) \
        + (HALO + tm) * (SSD_CONV_DIM + LRU_WIDTH) * 4 + 4 * tm * SSD_CONV_DIM * 4
    return pl.pallas_call(
        _inproj_kernel,
        grid=(b, s // tm),
        in_specs=[tok(D_MODEL), par(1, D_MODEL), par(D_MODEL, D_IN_PAD),
                  par(1, V7X_LANES), par(1, V7X_LANES), par(1, V7X_LANES),
                  par(CONV_WIDTH, SSD_CONV_DIM), par(1, SSD_CONV_DIM),
                  par(CONV_WIDTH, LRU_WIDTH), par(1, LRU_WIDTH)],
        out_specs=[head_qt, head, head_t] + [tok(wd) for wd in tok_widths],
        out_shape=[jax.ShapeDtypeStruct((b, ATT_HEADS, s // tm, V7X_LANES, tm), BF16),
                   jax.ShapeDtypeStruct((b, ATT_HEADS, s, V7X_LANES), BF16),
                   jax.ShapeDtypeStruct((b, ATT_HEADS, s // tm, ATT_VT_ROWS, tm), BF16)]
        + [jax.ShapeDtypeStruct((b, s, wd), F32) for wd in tok_widths],
        scratch_shapes=[pltpu.VMEM((1, V7X_LANES), F32),
                        pltpu.VMEM((HALO + tm, SSD_CONV_DIM), F32),
                        pltpu.VMEM((HALO + tm, LRU_WIDTH), F32),
                        pltpu.VMEM((tm, 3 * ATT_WIDTH + V7X_LANES), F32),
                        pltpu.VMEM((3, tm, V7X_LANES), F32)],
        compiler_params=_params(("parallel", "arbitrary"), nbytes),
        name="inproj",
    )(h, g, w, qg, kg, fb, ssd_cw, ssd_cb, lru_cw, lru_cb)


def _att_kernel(qt_ref, ka_ref, vt_ref, z_ref, o_ref, s_ref, cmax_ref, m_ref, acc_ref,
                *, tq, tk, tv, heads):
    qi = pl.program_id(2)
    subs = tq // tk
    tqb = qt_ref.shape[-1]
    assert tk % tqb == 0
    chains = [(sub, hh) for sub in range(subs) for hh in range(heads)]
    slots = ATT_LOOKAHEAD + 1

    m_ref[...] = jnp.full(m_ref.shape, -jnp.inf, F32)
    acc_ref[...] = jnp.zeros(acc_ref.shape, F32)

    assert len(chains) % slots == 0

    def causal_mask(s_t, sub, q0):
        key_r = lax.broadcasted_iota(jnp.int32, s_t.shape, 0)
        qry_c = lax.broadcasted_iota(jnp.int32, s_t.shape, 1)
        return jnp.where(key_r + (sub * tk - q0) <= qry_c, s_t, -jnp.inf)

    def issue_logits(j, idx, q0=0, diagonal=False):
        sub, hh = chains[idx]
        k0 = pl.multiple_of((j * subs + sub) * tk, tk)
        keys = ka_ref[0, hh, pl.ds(k0, tk), :]
        for qb in range(q0 // tqb, tq // tqb):
            cols = slice(qb * tqb, (qb + 1) * tqb)
            s_t = jnp.dot(keys, qt_ref[0, hh, qb], preferred_element_type=F32)
            if diagonal:
                s_t = causal_mask(s_t, sub, qb * tqb)
            s_ref[idx % slots, :, cols] = s_t
            cmax_ref[idx % slots, :, cols] = jnp.max(s_t, axis=0, keepdims=True)

    def step(j, diagonal):
        for idx, (sub, hh) in enumerate(chains):
            q0 = sub * tk if diagonal else 0
            s_t = s_ref[idx % slots, :, q0:]
            cmax = cmax_ref[idx % slots, :, q0:]
            ahead = idx + ATT_LOOKAHEAD
            if ahead < len(chains):
                issue_logits(j, ahead, chains[ahead][0] * tk if diagonal else 0, diagonal)
            elif not diagonal:
                issue_logits(j + 1, ahead - len(chains))
            if diagonal and idx < ATT_LOOKAHEAD:
                s_t = causal_mask(s_t, sub, q0)
                cmax = jnp.max(s_t, axis=0, keepdims=True)
            m = m_ref[hh, :, q0:]
            m_new = jnp.maximum(m, cmax)
            m_ref[hh, :, q0:] = m_new
            p_t = jnp.exp2(s_t - m_new).astype(BF16)
            acc = acc_ref[hh, :, q0:] * jnp.exp2(m - m_new)
            for vb in range(tk // tv):
                acc = acc + jnp.dot(vt_ref[0, hh, (j * subs + sub) * (tk // tv) + vb],
                                    p_t[vb * tv:(vb + 1) * tv, :], preferred_element_type=F32)
            acc_ref[hh, :, q0:] = acc

    for idx in range(ATT_LOOKAHEAD):
        issue_logits(0, idx)

    @pl.loop(0, qi)
    def _(j):
        step(j, False)

    step(qi, True)

    lead = ATT_VT_ROWS - ATT_HEAD_DIM
    for pr in range(heads // 2):
        even, odd = acc_ref[2 * pr], acc_ref[2 * pr + 1]
        o_t = jnp.concatenate(
            [even[0:ATT_HEAD_DIM, :] * (1.0 / even[ATT_HEAD_DIM:ATT_HEAD_DIM + 1, :]),
             odd[lead:, :] * (1.0 / odd[0:1, :])], axis=0)
        lanes = slice(V7X_LANES * pr, V7X_LANES * (pr + 1))
        o_ref[0, :, lanes] = (o_t.T * _silu(z_ref[0, :, lanes])).astype(o_ref.dtype)


def _attention(qt, ka, vt, z_att):
    b, _, s, _ = ka.shape
    tq, tk, hp = TQ_ATT, TK_ATT, ATT_HEADS_PER_STEP
    assert tq % tk == 0 and hp % 2 == 0 and ATT_HEADS % hp == 0
    wd = hp * ATT_HEAD_DIM
    tqb = qt.shape[-1]
    q_tile = pl.BlockSpec((1, hp, tq // tqb, V7X_LANES, tqb), lambda i, j, t: (i, j, t, 0, 0))
    k_full = pl.BlockSpec((1, hp, s, V7X_LANES), lambda i, j, t: (i, j, 0, 0))
    tv = vt.shape[-1]
    assert tk % tv == 0
    v_full = pl.BlockSpec((1, hp, s // tv, ATT_VT_ROWS, tv), lambda i, j, t: (i, j, 0, 0, 0))
    tok = pl.BlockSpec((1, tq, wd), lambda i, j, t: (i, t, j))
    nbytes = 2 * (hp * tq * V7X_LANES * 2 + 2 * hp * s * V7X_LANES * 2 + tq * wd * 6) \
        + (2 * ATT_LOOKAHEAD + 4) * tq * tk * 4
    return pl.pallas_call(
        functools.partial(_att_kernel, tq=tq, tk=tk, tv=tv, heads=hp),
        grid=(b, ATT_HEADS // hp, s // tq),
        in_specs=[q_tile, k_full, v_full, tok],
        out_specs=tok,
        out_shape=jax.ShapeDtypeStruct((b, s, ATT_WIDTH), BF16),
        scratch_shapes=[pltpu.VMEM((ATT_LOOKAHEAD + 1, tk, tq), F32),
                        pltpu.VMEM((ATT_LOOKAHEAD + 1, 1, tq), F32),
                        pltpu.VMEM((hp, 1, tq), F32),
                        pltpu.VMEM((hp, ATT_VT_ROWS, tq), F32)],
        compiler_params=_params(("parallel", "parallel", "arbitrary"), nbytes),
        name="att",
    )(qt, ka, vt, z_att)


def _ssd_kernel(xbc_ref, z_ref, dt_ref, dtb_ref, alog_ref, dsk_ref, ng_ref, y_ref, state_ref,
                *, chunks):
    L = SSD_CHUNK
    n = SSD_STATE
    gw = SSD_WIDTH // SSD_GROUPS

    @pl.when(pl.program_id(1) == 0)
    def _():
        state_ref[...] = jnp.zeros_like(state_ref)

    row = lax.broadcasted_iota(jnp.int32, (L, L), 0)
    col = lax.broadcasted_iota(jnp.int32, (L, L), 1)
    causal = col <= row
    tri = jnp.where(causal, 1.0, 0.0).astype(BF16)
    lo_half = lax.broadcasted_iota(jnp.int32, (L, V7X_LANES), 1) < SSD_HEAD_DIM
    a_neg = -jnp.exp(alog_ref[...])

    def decay_terms(ci):
        rows = slice(ci * L, (ci + 1) * L)
        dt = _softplus(dt_ref[0, rows, :] + dtb_ref[...])
        acs = _cumsum_rows(dt * a_neg, tri)
        tot = acs[L - 1:L, :]
        return (acs, jnp.exp(acs), jnp.exp(tot), acs.T, dt.T, (dt * jnp.exp(tot - acs)).T)

    terms = decay_terms(0)
    for ci in range(chunks):
        rows = slice(ci * L, (ci + 1) * L)
        acs, e_acs, e_tot, acs_t, dt_t, w_t = terms
        if ci + 1 < chunks:
            terms = decay_terms(ci + 1)

        y_tiles = []
        for g in range(SSD_GROUPS):
            b_lo = SSD_WIDTH + n * g
            c_lo = SSD_WIDTH + SSD_GROUPS * n + n * g
            bg_t = xbc_ref[0, rows, b_lo:b_lo + n].T
            cg = xbc_ref[0, rows, c_lo:c_lo + n].astype(BF16)
            cb = jnp.dot(cg, bg_t.astype(BF16), preferred_element_type=F32)
            s_prev = state_ref[g]
            cs = jnp.dot(cg, s_prev.astype(BF16), preferred_element_type=F32)
            for jj in range(gw // V7X_LANES):
                j = g * (gw // V7X_LANES) + jj
                he, ho = SSD_DT_LANE0 + 2 * j, SSD_DT_LANE0 + 2 * j + 1
                xp = xbc_ref[0, rows, V7X_LANES * j:V7X_LANES * (j + 1)]
                rhs = jnp.concatenate([jnp.where(lo_half, xp, 0.0), jnp.where(lo_half, 0.0, xp)],
                                      axis=0).astype(BF16)

                def scores(h):
                    seg = acs[:, h:h + 1] - acs_t[h:h + 1, :]
                    return cb * jnp.exp(jnp.where(causal, seg, -jnp.inf)) * dt_t[h:h + 1, :]

                lhs = jnp.concatenate([scores(he), scores(ho)], axis=1).astype(BF16)
                y_diag = jnp.dot(lhs, rhs, preferred_element_type=F32)
                e_pair = jnp.where(lo_half, e_acs[:, he:he + 1], e_acs[:, ho:ho + 1])
                y_off = cs[:, V7X_LANES * jj:V7X_LANES * (jj + 1)] * e_pair
                y_tiles.append(y_diag + y_off + xp * dsk_ref[:, V7X_LANES * j:V7X_LANES * (j + 1)])

                lhs_s = jnp.concatenate([bg_t * w_t[he:he + 1, :], bg_t * w_t[ho:ho + 1, :]],
                                        axis=1).astype(BF16)
                s_new = jnp.dot(lhs_s, rhs, preferred_element_type=F32)
                dec = jnp.where(lo_half[0:1], e_tot[:, he:he + 1], e_tot[:, ho:ho + 1])
                state_ref[g, :, V7X_LANES * jj:V7X_LANES * (jj + 1)] = (
                    s_prev[:, V7X_LANES * jj:V7X_LANES * (jj + 1)] * dec + s_new)

        y = jnp.concatenate(y_tiles, axis=1)
        gated = y * _silu(z_ref[0, rows, :])
        outs = []
        for g in range(SSD_GROUPS):
            gg = gated[:, gw * g:gw * (g + 1)]
            ms = jnp.mean(gg * gg, axis=-1, keepdims=True)
            outs.append(gg * lax.rsqrt(ms + EPS))
        y_ref[0, rows, :] = (jnp.concatenate(outs, axis=1) * ng_ref[...]).astype(y_ref.dtype)


def _ssd(xbc, z, dt_raw, dt_bias, a_log, d_skip, norm_g):
    b, s, _ = xbc.shape
    chunks = SSD_CHUNKS_PER_STEP
    ts = chunks * SSD_CHUNK
    tok = lambda wd: pl.BlockSpec((1, ts, wd), lambda i, t: (i, t, 0))
    par = lambda r, wd: pl.BlockSpec((r, wd), lambda i, t: (0, 0))
    nbytes = 2 * ts * (SSD_CONV_DIM + SSD_WIDTH + V7X_LANES) * 4 + 2 * ts * SSD_WIDTH * 2 \
        + SSD_GROUPS * SSD_STATE * SSD_WIDTH * 2 + 8 * SSD_CHUNK * SSD_CONV_DIM * 4
    return pl.pallas_call(
        functools.partial(_ssd_kernel, chunks=chunks),
        grid=(b, s // ts),
        in_specs=[tok(SSD_CONV_DIM), tok(SSD_WIDTH), tok(V7X_LANES),
                  par(1, V7X_LANES), par(1, V7X_LANES), par(1, SSD_WIDTH), par(1, SSD_WIDTH)],
        out_specs=tok(SSD_WIDTH),
        out_shape=jax.ShapeDtypeStruct((b, s, SSD_WIDTH), BF16),
        scratch_shapes=[pltpu.VMEM((SSD_GROUPS, SSD_STATE, SSD_WIDTH // SSD_GROUPS), F32)],
        compiler_params=_params(("parallel", "arbitrary"), nbytes),
        name="ssd",
    )(xbc, z, dt_raw, dt_bias, a_log, d_skip, norm_g)


def _lru_kernel(x_ref, z_ref, wg_ref, bg_ref, lam_ref, y_ref, a_ref, b_ref, h_ref):
    ts = x_ref.shape[1]
    groups = ts // V7X_SUBLANES
    first_tile = pl.program_id(1) == 0

    @pl.when(first_tile)
    def _():
        h_ref[...] = jnp.zeros_like(h_ref)

    xc = x_ref[0]
    gates = jnp.dot(xc.astype(BF16), wg_ref[...], preferred_element_type=F32) + bg_ref[...]
    r = _sigmoid(gates[:, :LRU_WIDTH])
    i = _sigmoid(gates[:, LRU_WIDTH:])
    a = jnp.exp2(r * ((-LRU_C * LOG2E) * _softplus(-lam_ref[...])))
    gap = 1.0 - a * a
    mult = jnp.where(gap > 0.0, gap * lax.rsqrt(gap), 0.0)
    seq_start = first_tile & (lax.broadcasted_iota(jnp.int32, (ts, LRU_WIDTH), 0) == 0)
    mult = jnp.where(seq_start, 1.0, mult)
    b = mult * (i * xc)

    aa = a.reshape(groups, V7X_SUBLANES, LRU_WIDTH)
    bb = b.reshape(groups, V7X_SUBLANES, LRU_WIDTH)
    sub = lax.broadcasted_iota(jnp.int32, (groups, V7X_SUBLANES, LRU_WIDTH), 1)
    for d in (1, 2, 4):
        keep = sub >= d
        bb = jnp.where(keep, aa * pltpu.roll(bb, d, axis=1) + bb, bb)
        aa = jnp.where(keep, aa * pltpu.roll(aa, d, axis=1), aa)
    a_ref[...] = aa.reshape(ts, LRU_WIDTH)
    b_ref[...] = bb.reshape(ts, LRU_WIDTH)

    def group(gi, h):
        r0 = pl.multiple_of(gi * V7X_SUBLANES, V7X_SUBLANES)
        rows = pl.ds(r0, V7X_SUBLANES)
        hh = a_ref[rows, :] * h + b_ref[rows, :]
        b_ref[rows, :] = hh
        return hh[V7X_SUBLANES - 1:V7X_SUBLANES, :]

    h_ref[...] = lax.fori_loop(0, groups, group, h_ref[...], unroll=LRU_UNROLL)
    y_ref[0] = (b_ref[...] * _silu(z_ref[0])).astype(y_ref.dtype)


def _lru(x_lru, z_lru, w_gate, b_gate, lam):
    b, s, _ = x_lru.shape
    ts = TS_LRU
    tok = pl.BlockSpec((1, ts, LRU_WIDTH), lambda i, t: (i, t, 0))
    par = lambda r, wd: pl.BlockSpec((r, wd), lambda i, t: (0, 0))
    nbytes = 2 * 2 * ts * LRU_WIDTH * 4 + 2 * ts * LRU_WIDTH * 2 + 2 * LRU_WIDTH * 2 * LRU_WIDTH * 2 \
        + 2 * ts * LRU_WIDTH * 4 + 8 * ts * LRU_WIDTH * 4
    return pl.pallas_call(
        _lru_kernel,
        grid=(b, s // ts),
        in_specs=[tok, tok, par(LRU_WIDTH, 2 * LRU_WIDTH), par(1, 2 * LRU_WIDTH), par(1, LRU_WIDTH)],
        out_specs=tok,
        out_shape=jax.ShapeDtypeStruct((b, s, LRU_WIDTH), BF16),
        scratch_shapes=[pltpu.VMEM((ts, LRU_WIDTH), F32),
                        pltpu.VMEM((ts, LRU_WIDTH), F32),
                        pltpu.VMEM((1, LRU_WIDTH), F32)],
        compiler_params=_params(("parallel", "arbitrary"), nbytes),
        name="lru",
    )(x_lru, z_lru, w_gate, b_gate, lam)


def _outproj_kernel(h_ref, ya_ref, ys_ref, yl_ref, w_ref, o_ref):
    a0, a1, a2 = ATT_WIDTH, ATT_WIDTH + SSD_WIDTH, MIX_WIDTH
    o_ref[...] = (h_ref[...]
                  + jnp.dot(ya_ref[...], w_ref[0:a0, :], preferred_element_type=F32)
                  + jnp.dot(ys_ref[...], w_ref[a0:a1, :], preferred_element_type=F32)
                  + jnp.dot(yl_ref[...], w_ref[a1:a2, :], preferred_element_type=F32))


def _outproj(h, y_att, y_ssd, y_lru, w):
    t = h.shape[0]
    tm = TM_OUTPROJ
    tok = lambda wd: pl.BlockSpec((tm, wd), lambda i: (i, 0))
    nbytes = 2 * (2 * tm * D_MODEL * 4 + tm * MIX_WIDTH * 2 + MIX_WIDTH * D_MODEL * 2)
    return pl.pallas_call(
        _outproj_kernel,
        grid=(t // tm,),
        in_specs=[tok(D_MODEL), tok(ATT_WIDTH), tok(SSD_WIDTH), tok(LRU_WIDTH),
                  pl.BlockSpec((MIX_WIDTH, D_MODEL), lambda i: (0, 0))],
        out_specs=tok(D_MODEL),
        out_shape=jax.ShapeDtypeStruct((t, D_MODEL), F32),
        compiler_params=_params(("parallel",), nbytes),
        name="outproj",
    )(h, y_att, y_ssd, y_lru, w)


def _pad_lanes(v, lane0=0, width=V7X_LANES):
    return jnp.pad(v, (lane0, width - lane0 - v.shape[0]))[None, :]


_IN_OFF = {name: sum(IN_SIZES[:i]) for i, name in enumerate(IN_NAMES)}


def _regroup_kernel(w_ref, o_ref):
    w = w_ref[0]
    out = 0
    for run in _RUNS[:-1]:
        width = sum(wd for _, wd in run)
        src = _IN_OFF[run[0][0]]
        o_ref[:, out:out + width] = w[:, src:src + width].astype(BF16)
        out += width
    f0, d0 = _IN_OFF["f_raw"], _IN_OFF["dt_raw"] - SSD_DT_LANE0
    lane = lax.broadcasted_iota(jnp.int32, (w.shape[0], V7X_LANES), 1)
    fdt = jnp.where(lane < SSD_DT_LANE0, w[:, f0:f0 + V7X_LANES],
                    jnp.where(lane < SSD_DT_LANE0 + SSD_HEADS, w[:, d0:d0 + V7X_LANES], 0.0))
    o_ref[:, out:] = fdt.astype(BF16)


def _regroup_w_in(w_in, layer):
    assert [n for run in _RUNS[:-1] for n, _ in run] == [n for n in IN_NAMES
                                                          if n not in ("f_raw", "dt_raw")]
    assert _IN_OFF["f_raw"] % V7X_LANES == 0 and (_IN_OFF["dt_raw"] - SSD_DT_LANE0) % V7X_LANES == 0
    assert IN_SIZES[IN_NAMES.index("f_raw")] == SSD_DT_LANE0
    d_in = w_in.shape[-1]
    tr = TR_REGROUP
    nbytes = 2 * tr * (d_in * 4 + D_IN_PAD * 2) + 2 * tr * d_in * 4
    return pl.pallas_call(
        _regroup_kernel,
        grid=(D_MODEL // tr,),
        in_specs=[pl.BlockSpec((1, tr, d_in), lambda i: (layer, i, 0))],
        out_specs=pl.BlockSpec((tr, D_IN_PAD), lambda i: (i, 0)),
        out_shape=jax.ShapeDtypeStruct((D_MODEL, D_IN_PAD), BF16),
        compiler_params=_params(("parallel",), nbytes),
        name="regroup",
    )(w_in)


def _block_diag(w):
    nb, d, e = w.shape
    tiled = jnp.tile(w.reshape(nb * d, e), (1, nb))
    row = lax.broadcasted_iota(jnp.int32, tiled.shape, 0) // d
    col = lax.broadcasted_iota(jnp.int32, tiled.shape, 1) // e
    return jnp.where(row == col, tiled, 0.0)


def kernel(x, norm_g, w_in, q_norm_g, k_norm_g, forget_b, ssd_conv_w, ssd_conv_b, ssd_dt_bias,
           ssd_a_log, ssd_d, ssd_norm_g, lru_conv_w, lru_conv_b, lru_w_a, lru_b_a, lru_w_x, lru_b_x,
           lru_lambda, w_out):
    bsz, seq, d = x.shape
    assert d == D_MODEL and seq % max(TQ_ATT, TM_INPROJ, TS_LRU, SSD_CHUNKS_PER_STEP * SSD_CHUNK) == 0
    assert (bsz * seq) % TM_OUTPROJ == 0
    depth = w_in.shape[0]
    h = x.astype(F32).reshape(bsz * seq, D_MODEL)
    for l in range(depth):
        qt, ka, vt, z_att, z_ssd, xbc, x_lru, z_lru, dt_raw = _inproj(
            h.reshape(bsz, seq, D_MODEL), norm_g[l][None, :], _regroup_w_in(w_in, l),
            jnp.tile(q_norm_g[l], 2)[None, :], jnp.tile(k_norm_g[l], 2)[None, :],
            _pad_lanes(forget_b[l]), ssd_conv_w[l], ssd_conv_b[l][None, :],
            lru_conv_w[l], lru_conv_b[l][None, :])
        y_att = _attention(qt, ka, vt, z_att)
        y_ssd = _ssd(xbc, z_ssd, dt_raw, _pad_lanes(ssd_dt_bias[l], SSD_DT_LANE0),
                     _pad_lanes(ssd_a_log[l], SSD_DT_LANE0),
                     jnp.repeat(ssd_d[l], SSD_HEAD_DIM)[None, :], ssd_norm_g[l][None, :])
        b_gate = jnp.concatenate([lru_b_a[l], lru_b_x[l]])[None, :]
        w_gate = jnp.concatenate([_block_diag(lru_w_a[l]), _block_diag(lru_w_x[l])], axis=1).astype(BF16)
        y_lru = _lru(x_lru, z_lru, w_gate, b_gate, lru_lambda[l][None, :])
        h = _outproj(h, y_att.reshape(bsz * seq, ATT_WIDTH), y_ssd.reshape(bsz * seq, SSD_WIDTH),
                     y_lru.reshape(bsz * seq, LRU_WIDTH), w_out[l].astype(BF16))
    return h.reshape(bsz, seq, D_MODEL).astype(x.dtype)
```

```python
import functools

import jax
import jax.numpy as jnp
from jax import lax
from jax.experimental import pallas as pl
from jax.experimental.pallas import tpu as pltpu

F32 = jnp.float32
BF16 = jnp.bfloat16

V7X_LANES = 128
V7X_SUBLANES = 8
V7X_VMEM_BYTES = 64 * 1024 * 1024

D_MODEL = 1024
EPS = 1e-6
CONV_WIDTH = 4
MIX_WIDTH = 2 * D_MODEL
ATT_HEAD_DIM = 64
ATT_WIDTH = MIX_WIDTH // 4
ATT_HEADS = ATT_WIDTH // ATT_HEAD_DIM
SSD_HEAD_DIM = 64
SSD_WIDTH = MIX_WIDTH // 2
SSD_HEADS = SSD_WIDTH // SSD_HEAD_DIM
SSD_GROUPS = 2
SSD_STATE = 128
SSD_CHUNK = 128
SSD_CONV_DIM = SSD_WIDTH + 2 * SSD_GROUPS * SSD_STATE
LRU_WIDTH = MIX_WIDTH // 4
LRU_BLOCKS = 8
LRU_C = 8.0
IN_SIZES = (ATT_WIDTH, ATT_WIDTH, ATT_WIDTH, ATT_WIDTH, ATT_HEADS,
            SSD_WIDTH, SSD_CONV_DIM, SSD_HEADS, LRU_WIDTH, LRU_WIDTH)
IN_NAMES = ("q", "k", "v", "z_att", "f_raw", "z_ssd", "xbc", "dt_raw", "x_lru", "z_lru")

LOG2E = 1.4426950408889634
HALO = V7X_SUBLANES

SSD_DT_LANE0 = ATT_HEADS
_RUNS = ((("q", ATT_WIDTH), ("k", ATT_WIDTH), ("v", ATT_WIDTH), ("z_att", ATT_WIDTH)),
         (("z_ssd", SSD_WIDTH), ("xbc", SSD_CONV_DIM)),
         (("x_lru", LRU_WIDTH), ("z_lru", LRU_WIDTH)),
         (("fdt", V7X_LANES),))
_SEGS = tuple(seg for run in _RUNS for seg in run)
_SEG = {name: (sum(w for _, w in _SEGS[:si]), sum(w for _, w in _SEGS[:si + 1]))
        for si, (name, _) in enumerate(_SEGS)}
D_IN_PAD = sum(w for _, w in _SEGS)

TQ_ATT = 1024
TK_ATT = 256
ATT_VT_ROWS = 80
ATT_HEADS_PER_STEP = 8
ATT_LOOKAHEAD = 3
TM_INPROJ = 256
INPROJ_CHUNK = 256
SSD_CHUNKS_PER_STEP = 2
TS_LRU = 1024
LRU_UNROLL = 4
TM_OUTPROJ = 1024
TR_REGROUP = 128


def _vmem_limit(nbytes):
    return int(min(nbytes * 3 // 2 + (8 << 20), V7X_VMEM_BYTES - (6 << 20)))


def _params(semantics, nbytes):
    return pltpu.CompilerParams(dimension_semantics=semantics, vmem_limit_bytes=_vmem_limit(nbytes))


def _sigmoid(x):
    return 1.0 / (1.0 + jnp.exp(-x))


def _silu(x):
    return x * _sigmoid(x)


def _softplus(x):
    return jnp.maximum(x, 0.0) + jnp.log(1.0 + jnp.exp(-jnp.abs(x)))


def _split3(x):
    hi = x.astype(BF16)
    r1 = x - hi.astype(F32)
    mid = r1.astype(BF16)
    lo = (r1 - mid.astype(F32)).astype(BF16)
    return hi, mid, lo


def _tril_ones(n):
    row = lax.broadcasted_iota(jnp.int32, (n, n), 0)
    col = lax.broadcasted_iota(jnp.int32, (n, n), 1)
    return jnp.where(col <= row, 1.0, 0.0).astype(BF16)


def _cumsum_rows(x, tri):
    hi, mid, lo = _split3(x)
    dot = lambda a: jnp.dot(tri, a, preferred_element_type=F32)
    return dot(hi) + dot(mid) + dot(lo)


def _causal_conv(pad_ref, w_ref, b_ref, cols):
    rows = pad_ref.shape[0] - HALO
    xp = pad_ref[:, cols]
    x = xp[HALO:, :]
    y = b_ref[:, cols] + w_ref[CONV_WIDTH - 1:CONV_WIDTH, cols] * x
    for shift in range(1, CONV_WIDTH):
        kk = CONV_WIDTH - 1 - shift
        y = y + w_ref[kk:kk + 1, cols] * pltpu.roll(xp, shift, axis=0)[HALO:, :]
    pad_ref[0:HALO, cols] = x[rows - HALO:rows, :]
    return y


def _forget_bias_terms(qkvf_ref, fb, cum_ref, bias_ref):
    ts = qkvf_ref.shape[0]
    log_f = -_softplus(-(qkvf_ref[:, 3 * ATT_WIDTH:] + fb))
    c = cum_ref[...] + _cumsum_rows(log_f, _tril_ones(ts))
    cum_ref[...] = c[ts - 1:ts, :]
    for t, term in enumerate(_split3(c * (-LOG2E))):
        bias_ref[t] = term.astype(F32)


def _att_operands_pair(j, qkvf_ref, bias_ref, qg, kg, qt_ref, ka_ref, vt_ref):
    ts = qkvf_ref.shape[0]
    lane = lax.broadcasted_iota(jnp.int32, (ts, V7X_LANES), 1)
    lo_half = lane < ATT_HEAD_DIM
    q_scale = ATT_HEAD_DIM ** -0.5 * LOG2E

    def normed(x, g):
        sq = x * x
        ss_lo = jnp.sum(jnp.where(lo_half, sq, 0.0), axis=1, keepdims=True)
        ss_hi = jnp.sum(jnp.where(lo_half, 0.0, sq), axis=1, keepdims=True)
        inv = jnp.where(lo_half, lax.rsqrt(ss_lo / ATT_HEAD_DIM + EPS),
                        lax.rsqrt(ss_hi / ATT_HEAD_DIM + EPS))
        return x * inv * g

    lo, hi = V7X_LANES * j, V7X_LANES * (j + 1)
    qn = normed(qkvf_ref[:, lo:hi], qg) * q_scale
    kn = normed(qkvf_ref[:, ATT_WIDTH + lo:ATT_WIDTH + hi], kg)
    vp_t = qkvf_ref[:, 2 * ATT_WIDTH + lo:2 * ATT_WIDTH + hi].T
    pad_rows = ATT_VT_ROWS - ATT_HEAD_DIM
    ones_row = jnp.where(lax.broadcasted_iota(jnp.int32, (pad_rows, ts), 0) == 0, 1.0, 0.0)
    vt_ref[0, 2 * j, 0] = jnp.concatenate([vp_t[:ATT_HEAD_DIM, :], ones_row], axis=0).astype(BF16)
    vt_ref[0, 2 * j + 1, 0] = jnp.concatenate([ones_row, vp_t[ATT_HEAD_DIM:, :]], axis=0).astype(BF16)
    qn_t = qn.T
    ones3_rows = jnp.where(lax.broadcasted_iota(jnp.int32, (ATT_HEAD_DIM, ts), 0) < 3, 1.0, 0.0)
    qt_ref[0, 2 * j, 0] = jnp.concatenate([qn_t[:ATT_HEAD_DIM, :], ones3_rows], axis=0).astype(BF16)
    qt_ref[0, 2 * j + 1, 0] = jnp.concatenate([qn_t[ATT_HEAD_DIM:, :], ones3_rows], axis=0).astype(BF16)
    for parity in range(2):
        h = 2 * j + parity
        kh = kn if parity == 0 else pltpu.roll(kn, ATT_HEAD_DIM, axis=1)
        bias = jnp.where(lane == ATT_HEAD_DIM, bias_ref[0, :, h:h + 1],
                         jnp.where(lane == ATT_HEAD_DIM + 1, bias_ref[1, :, h:h + 1],
                                   jnp.where(lane == ATT_HEAD_DIM + 2, bias_ref[2, :, h:h + 1], 0.0)))
        ka_ref[0, h] = jnp.where(lo_half, kh, bias).astype(BF16)


def _inproj_kernel(x_ref, g_ref, w_ref, qg_ref, kg_ref, fb_ref, scw_ref, scb_ref, lcw_ref, lcb_ref,
                   qt_ref, ka_ref, vt_ref, zatt_ref, zssd_ref, xbc_ref, xlru_ref, zlru_ref, dt_ref,
                   cum_ref, spad_ref, lpad_ref, qkvf_ref, bias_ref):
    tm = x_ref.shape[1]

    @pl.when(pl.program_id(1) == 0)
    def _():
        cum_ref[...] = jnp.zeros_like(cum_ref)
        spad_ref[0:HALO, :] = jnp.zeros((HALO, SSD_CONV_DIM), F32)
        lpad_ref[0:HALO, :] = jnp.zeros((HALO, LRU_WIDTH), F32)

    x = x_ref[0]
    ms = jnp.mean(x * x, axis=-1, keepdims=True)
    u = (x * lax.rsqrt(ms + EPS) * g_ref[...]).astype(BF16)

    def proj(name, cols=None):
        lo, hi = _SEG[name]
        if cols is not None:
            lo, hi = lo + cols.start, lo + cols.stop
        return jnp.dot(u, w_ref[:, lo:hi], preferred_element_type=F32)

    chunk = lambda c: slice(c * INPROJ_CHUNK, (c + 1) * INPROJ_CHUNK)
    plain = [(ref, name, chunk(c))
             for ref, name, width in ((zssd_ref, "z_ssd", SSD_WIDTH), (zatt_ref, "z_att", ATT_WIDTH),
                                      (zlru_ref, "z_lru", LRU_WIDTH))
             for c in range(width // INPROJ_CHUNK)]
    convs = [(spad_ref, scw_ref, scb_ref, xbc_ref, "xbc", chunk(c), True)
             for c in range(SSD_CONV_DIM // INPROJ_CHUNK)]
    convs += [(lpad_ref, lcw_ref, lcb_ref, xlru_ref, "x_lru", chunk(c), False)
              for c in range(LRU_WIDTH // INPROJ_CHUNK)]

    def matmul_plain():
        if plain:
            ref, name, cols = plain.pop(0)
            ref[0, :, cols] = proj(name, cols)

    def matmul_conv(ci):
        if ci < len(convs):
            pad_ref, _, _, _, name, cols, _ = convs[ci]
            pad_ref[HALO:HALO + tm, cols] = proj(name, cols)

    def finish_conv(ci):
        pad_ref, cw_ref, cb_ref, out_ref, _, cols, silu = convs[ci]
        y = _causal_conv(pad_ref, cw_ref, cb_ref, cols)
        out_ref[0, :, cols] = _silu(y) if silu else y

    for si, name in enumerate(("q", "k", "v")):
        qkvf_ref[:, si * ATT_WIDTH:(si + 1) * ATT_WIDTH] = proj(name)
    fdt = proj("fdt")
    qkvf_ref[:, 3 * ATT_WIDTH:] = fdt
    dt_ref[0] = fdt
    matmul_conv(0)
    matmul_plain()
    _forget_bias_terms(qkvf_ref, fb_ref[...], cum_ref, bias_ref)
    pairs = list(range(ATT_HEADS // 2))
    for ci in range(len(convs)):
        matmul_conv(ci + 1)
        if pairs:
            _att_operands_pair(pairs.pop(0), qkvf_ref, bias_ref, qg_ref[...], kg_ref[...],
                               qt_ref, ka_ref, vt_ref)
        matmul_plain()
        finish_conv(ci)
    assert not pairs and not plain


def _inproj(h, g, w, qg, kg, fb, ssd_cw, ssd_cb, lru_cw, lru_cb):
    b, s, _ = h.shape
    tm = TM_INPROJ
    tok = lambda wd: pl.BlockSpec((1, tm, wd), lambda i, t: (i, t, 0))
    par = lambda r, wd: pl.BlockSpec((r, wd), lambda i, t: (0, 0))
    head = pl.BlockSpec((1, ATT_HEADS, tm, V7X_LANES), lambda i, t: (i, 0, t, 0))
    head_t = pl.BlockSpec((1, ATT_HEADS, 1, ATT_VT_ROWS, tm), lambda i, t: (i, 0, t, 0, 0))
    head_qt = pl.BlockSpec((1, ATT_HEADS, 1, V7X_LANES, tm), lambda i, t: (i, 0, t, 0, 0))
    tok_widths = (ATT_WIDTH, SSD_WIDTH, SSD_CONV_DIM, LRU_WIDTH, LRU_WIDTH, V7X_LANES)
    nbytes = 2 * (tm * D_MODEL * 4 + D_MODEL * D_IN_PAD * 2 + tm * sum(tok_widths) * 4
                  + 3 * ATT_HEADS * tm * V7X_LANES * 2) \
        + (HALO + tm) * (SSD_CONV_DIM + LRU_WIDTH) * 4 + 4 * tm * SSD_CONV_DIM * 4
    return pl.pallas_call(
        _inproj_kernel,
        grid=(b, s // tm),
        in_specs=[tok(D_MODEL), par(1, D_MODEL), par(D_MODEL, D_IN_PAD),
                  par(1, V7X_LANES), par(1, V7X_LANES), par(1, V7X_LANES),
                  par(CONV_WIDTH, SSD_CONV_DIM), par(1, SSD_CONV_DIM),
                  par(CONV_WIDTH, LRU_WIDTH), par(1, LRU_WIDTH)],
        out_specs=[head_qt, head, head_t] + [tok(wd) for wd in tok_widths],
        out_shape=[jax.ShapeDtypeStruct((b, ATT_HEADS, s // tm, V7X_LANES, tm), BF16),
                   jax.ShapeDtypeStruct((b, ATT_HEADS, s, V7X_LANES), BF16),
                   jax.ShapeDtypeStruct((b, ATT_HEADS, s // tm, ATT_VT_ROWS, tm), BF16)]
        + [jax.ShapeDtypeStruct((b, s, wd), F32) for wd in tok_widths],
        scratch_shapes=[pltpu.VMEM((1, V7X_LANES), F32),
                        pltpu.VMEM((HALO + tm, SSD_CONV_DIM), F32),
                        pltpu.VMEM((HALO + tm, LRU_WIDTH), F32),
                        pltpu.VMEM((tm, 3 * ATT_WIDTH + V7X_LANES), F32),
                        pltpu.VMEM((3, tm, V7X_LANES), F32)],
        compiler_params=_params(("parallel", "arbitrary"), nbytes),
        name="inproj",
    )(h, g, w, qg, kg, fb, ssd_cw, ssd_cb, lru_cw, lru_cb)


def _att_kernel(qt_ref, ka_ref, vt_ref, z_ref, o_ref, s_ref, cmax_ref, m_ref, acc_ref,
                *, tq, tk, tv, heads):
    qi = pl.program_id(2)
    subs = tq // tk
    tqb = qt_ref.shape[-1]
    assert tk % tqb == 0
    chains = [(sub, hh) for sub in range(subs) for hh in range(heads)]
    slots = ATT_LOOKAHEAD + 1

    m_ref[...] = jnp.full(m_ref.shape, -jnp.inf, F32)
    acc_ref[...] = jnp.zeros(acc_ref.shape, F32)

    assert len(chains) % slots == 0

    def causal_mask(s_t, sub, q0):
        key_r = lax.broadcasted_iota(jnp.int32, s_t.shape, 0)
        qry_c = lax.broadcasted_iota(jnp.int32, s_t.shape, 1)
        return jnp.where(key_r + (sub * tk - q0) <= qry_c, s_t, -jnp.inf)

    def issue_logits(j, idx, q0=0, diagonal=False):
        sub, hh = chains[idx]
        k0 = pl.multiple_of((j * subs + sub) * tk, tk)
        keys = ka_ref[0, hh, pl.ds(k0, tk), :]
        for qb in range(q0 // tqb, tq // tqb):
            cols = slice(qb * tqb, (qb + 1) * tqb)
            s_t = jnp.dot(keys, qt_ref[0, hh, qb], preferred_element_type=F32)
            if diagonal:
                s_t = causal_mask(s_t, sub, qb * tqb)
            s_ref[idx % slots, :, cols] = s_t
            cmax_ref[idx % slots, :, cols] = jnp.max(s_t, axis=0, keepdims=True)

    def step(j, diagonal):
        for idx, (sub, hh) in enumerate(chains):
            q0 = sub * tk if diagonal else 0
            s_t = s_ref[idx % slots, :, q0:]
            cmax = cmax_ref[idx % slots, :, q0:]
            ahead = idx + ATT_LOOKAHEAD
            if ahead < len(chains):
                issue_logits(j, ahead, chains[ahead][0] * tk if diagonal else 0, diagonal)
            elif not diagonal:
                issue_logits(j + 1, ahead - len(chains))
            if diagonal and idx < ATT_LOOKAHEAD:
                s_t = causal_mask(s_t, sub, q0)
                cmax = jnp.max(s_t, axis=0, keepdims=True)
            m = m_ref[hh, :, q0:]
            m_new = jnp.maximum(m, cmax)
            m_ref[hh, :, q0:] = m_new
            p_t = jnp.exp2(s_t - m_new).astype(BF16)
            acc = acc_ref[hh, :, q0:] * jnp.exp2(m - m_new)
            for vb in range(tk // tv):
                acc = acc + jnp.dot(vt_ref[0, hh, (j * subs + sub) * (tk // tv) + vb],
                                    p_t[vb * tv:(vb + 1) * tv, :], preferred_element_type=F32)
            acc_ref[hh, :, q0:] = acc

    for idx in range(ATT_LOOKAHEAD):
        issue_logits(0, idx)

    @pl.loop(0, qi)
    def _(j):
        step(j, False)

    step(qi, True)

    lead = ATT_VT_ROWS - ATT_HEAD_DIM
    for pr in range(heads // 2):
        even, odd = acc_ref[2 * pr], acc_ref[2 * pr + 1]
        o_t = jnp.concatenate(
            [even[0:ATT_HEAD_DIM, :] * (1.0 / even[ATT_HEAD_DIM:ATT_HEAD_DIM + 1, :]),
             odd[lead:, :] * (1.0 / odd[0:1, :])], axis=0)
        lanes = slice(V7X_LANES * pr, V7X_LANES * (pr + 1))
        o_ref[0, :, lanes] = (o_t.T * _silu(z_ref[0, :, lanes])).astype(o_ref.dtype)


def _attention(qt, ka, vt, z_att):
    b, _, s, _ = ka.shape
    tq, tk, hp = TQ_ATT, TK_ATT, ATT_HEADS_PER_STEP
    assert tq % tk == 0 and hp % 2 == 0 and ATT_HEADS % hp == 0
    wd = hp * ATT_HEAD_DIM
    tqb = qt.shape[-1]
    q_tile = pl.BlockSpec((1, hp, tq // tqb, V7X_LANES, tqb), lambda i, j, t: (i, j, t, 0, 0))
    k_full = pl.BlockSpec((1, hp, s, V7X_LANES), lambda i, j, t: (i, j, 0, 0))
    tv = vt.shape[-1]
    assert tk % tv == 0
    v_full = pl.BlockSpec((1, hp, s // tv, ATT_VT_ROWS, tv), lambda i, j, t: (i, j, 0, 0, 0))
    tok = pl.BlockSpec((1, tq, wd), lambda i, j, t: (i, t, j))
    nbytes = 2 * (hp * tq * V7X_LANES * 2 + 2 * hp * s * V7X_LANES * 2 + tq * wd * 6) \
        + (2 * ATT_LOOKAHEAD + 4) * tq * tk * 4
    return pl.pallas_call(
        functools.partial(_att_kernel, tq=tq, tk=tk, tv=tv, heads=hp),
        grid=(b, ATT_HEADS // hp, s // tq),
        in_specs=[q_tile, k_full, v_full, tok],
        out_specs=tok,
        out_shape=jax.ShapeDtypeStruct((b, s, ATT_WIDTH), BF16),
        scratch_shapes=[pltpu.VMEM((ATT_LOOKAHEAD + 1, tk, tq), F32),
                        pltpu.VMEM((ATT_LOOKAHEAD + 1, 1, tq), F32),
                        pltpu.VMEM((hp, 1, tq), F32),
                        pltpu.VMEM((hp, ATT_VT_ROWS, tq), F32)],
        compiler_params=_params(("parallel", "parallel", "arbitrary"), nbytes),
        name="att",
    )(qt, ka, vt, z_att)


def _ssd_kernel(xbc_ref, z_ref, dt_ref, dtb_ref, alog_ref, dsk_ref, ng_ref, y_ref, state_ref,
                *, chunks):
    L = SSD_CHUNK
    n = SSD_STATE
    gw = SSD_WIDTH // SSD_GROUPS

    @pl.when(pl.program_id(1) == 0)
    def _():
        state_ref[...] = jnp.zeros_like(state_ref)

    row = lax.broadcasted_iota(jnp.int32, (L, L), 0)
    col = lax.broadcasted_iota(jnp.int32, (L, L), 1)
    causal = col <= row
    tri = jnp.where(causal, 1.0, 0.0).astype(BF16)
    lo_half = lax.broadcasted_iota(jnp.int32, (L, V7X_LANES), 1) < SSD_HEAD_DIM
    a_neg = -jnp.exp(alog_ref[...])

    def decay_terms(ci):
        rows = slice(ci * L, (ci + 1) * L)
        dt = _softplus(dt_ref[0, rows, :] + dtb_ref[...])
        acs = _cumsum_rows(dt * a_neg, tri)
        tot = acs[L - 1:L, :]
        return (acs, jnp.exp(acs), jnp.exp(tot), acs.T, dt.T, (dt * jnp.exp(tot - acs)).T)

    terms = decay_terms(0)
    for ci in range(chunks):
        rows = slice(ci * L, (ci + 1) * L)
        acs, e_acs, e_tot, acs_t, dt_t, w_t = terms
        if ci + 1 < chunks:
            terms = decay_terms(ci + 1)

        y_tiles = []
        for g in range(SSD_GROUPS):
            b_lo = SSD_WIDTH + n * g
            c_lo = SSD_WIDTH + SSD_GROUPS * n + n * g
            bg_t = xbc_ref[0, rows, b_lo:b_lo + n].T
            cg = xbc_ref[0, rows, c_lo:c_lo + n].astype(BF16)
            cb = jnp.dot(cg, bg_t.astype(BF16), preferred_element_type=F32)
            s_prev = state_ref[g]
            cs = jnp.dot(cg, s_prev.astype(BF16), preferred_element_type=F32)
            for jj in range(gw // V7X_LANES):
                j = g * (gw // V7X_LANES) + jj
                he, ho = SSD_DT_LANE0 + 2 * j, SSD_DT_LANE0 + 2 * j + 1
                xp = xbc_ref[0, rows, V7X_LANES * j:V7X_LANES * (j + 1)]
                rhs = jnp.concatenate([jnp.where(lo_half, xp, 0.0), jnp.where(lo_half, 0.0, xp)],
                                      axis=0).astype(BF16)

                def scores(h):
                    seg = acs[:, h:h + 1] - acs_t[h:h + 1, :]
                    return cb * jnp.exp(jnp.where(causal, seg, -jnp.inf)) * dt_t[h:h + 1, :]

                lhs = jnp.concatenate([scores(he), scores(ho)], axis=1).astype(BF16)
                y_diag = jnp.dot(lhs, rhs, preferred_element_type=F32)
                e_pair = jnp.where(lo_half, e_acs[:, he:he + 1], e_acs[:, ho:ho + 1])
                y_off = cs[:, V7X_LANES * jj:V7X_LANES * (jj + 1)] * e_pair
                y_tiles.append(y_diag + y_off + xp * dsk_ref[:, V7X_LANES * j:V7X_LANES * (j + 1)])

                lhs_s = jnp.concatenate([bg_t * w_t[he:he + 1, :], bg_t * w_t[ho:ho + 1, :]],
                                        axis=1).astype(BF16)
                s_new = jnp.dot(lhs_s, rhs, preferred_element_type=F32)
                dec = jnp.where(lo_half[0:1], e_tot[:, he:he + 1], e_tot[:, ho:ho + 1])
                state_ref[g, :, V7X_LANES * jj:V7X_LANES * (jj + 1)] = (
                    s_prev[:, V7X_LANES * jj:V7X_LANES * (jj + 1)] * dec + s_new)

        y = jnp.concatenate(y_tiles, axis=1)
        gated = y * _silu(z_ref[0, rows, :])
        outs = []
        for g in range(SSD_GROUPS):
            gg = gated[:, gw * g:gw * (g + 1)]
            ms = jnp.mean(gg * gg, axis=-1, keepdims=True)
            outs.append(gg * lax.rsqrt(ms + EPS))
        y_ref[0, rows, :] = (jnp.concatenate(outs, axis=1) * ng_ref[...]).astype(y_ref.dtype)


def _ssd(xbc, z, dt_raw, dt_bias, a_log, d_skip, norm_g):
    b, s, _ = xbc.shape
    chunks = SSD_CHUNKS_PER_STEP
    ts = chunks * SSD_CHUNK
    tok = lambda wd: pl.BlockSpec((1, ts, wd), lambda i, t: (i, t, 0))
    par = lambda r, wd: pl.BlockSpec((r, wd), lambda i, t: (0, 0))
    nbytes = 2 * ts * (SSD_CONV_DIM + SSD_WIDTH + V7X_LANES) * 4 + 2 * ts * SSD_WIDTH * 2 \
        + SSD_GROUPS * SSD_STATE * SSD_WIDTH * 2 + 8 * SSD_CHUNK * SSD_CONV_DIM * 4
    return pl.pallas_call(
        functools.partial(_ssd_kernel, chunks=chunks),
        grid=(b, s // ts),
        in_specs=[tok(SSD_CONV_DIM), tok(SSD_WIDTH), tok(V7X_LANES),
                  par(1, V7X_LANES), par(1, V7X_LANES), par(1, SSD_WIDTH), par(1, SSD_WIDTH)],
        out_specs=tok(SSD_WIDTH),
        out_shape=jax.ShapeDtypeStruct((b, s, SSD_WIDTH), BF16),
        scratch_shapes=[pltpu.VMEM((SSD_GROUPS, SSD_STATE, SSD_WIDTH // SSD_GROUPS), F32)],
        compiler_params=_params(("parallel", "arbitrary"), nbytes),
        name="ssd",
    )(xbc, z, dt_raw, dt_bias, a_log, d_skip, norm_g)


def _lru_kernel(x_ref, z_ref, wg_ref, bg_ref, lam_ref, y_ref, a_ref, b_ref, h_ref):
    ts = x_ref.shape[1]
    groups = ts // V7X_SUBLANES
    first_tile = pl.program_id(1) == 0

    @pl.when(first_tile)
    def _():
        h_ref[...] = jnp.zeros_like(h_ref)

    xc = x_ref[0]
    gates = jnp.dot(xc.astype(BF16), wg_ref[...], preferred_element_type=F32) + bg_ref[...]
    r = _sigmoid(gates[:, :LRU_WIDTH])
    i = _sigmoid(gates[:, LRU_WIDTH:])
    a = jnp.exp2(r * ((-LRU_C * LOG2E) * _softplus(-lam_ref[...])))
    gap = 1.0 - a * a
    mult = jnp.where(gap > 0.0, gap * lax.rsqrt(gap), 0.0)
    seq_start = first_tile & (lax.broadcasted_iota(jnp.int32, (ts, LRU_WIDTH), 0) == 0)
    mult = jnp.where(seq_start, 1.0, mult)
    b = mult * (i * xc)

    aa = a.reshape(groups, V7X_SUBLANES, LRU_WIDTH)
    bb = b.reshape(groups, V7X_SUBLANES, LRU_WIDTH)
    sub = lax.broadcasted_iota(jnp.int32, (groups, V7X_SUBLANES, LRU_WIDTH), 1)
    for d in (1, 2, 4):
        keep = sub >= d
        bb = jnp.where(keep, aa * pltpu.roll(bb, d, axis=1) + bb, bb)
        aa = jnp.where(keep, aa * pltpu.roll(aa, d, axis=1), aa)
    a_ref[...] = aa.reshape(ts, LRU_WIDTH)
    b_ref[...] = bb.reshape(ts, LRU_WIDTH)

    def group(gi, h):
        r0 = pl.multiple_of(gi * V7X_SUBLANES, V7X_SUBLANES)
        rows = pl.ds(r0, V7X_SUBLANES)
        hh = a_ref[rows, :] * h + b_ref[rows, :]
        b_ref[rows, :] = hh
        return hh[V7X_SUBLANES - 1:V7X_SUBLANES, :]

    h_ref[...] = lax.fori_loop(0, groups, group, h_ref[...], unroll=LRU_UNROLL)
    y_ref[0] = (b_ref[...] * _silu(z_ref[0])).astype(y_ref.dtype)


def _lru(x_lru, z_lru, w_gate, b_gate, lam):
    b, s, _ = x_lru.shape
    ts = TS_LRU
    tok = pl.BlockSpec((1, ts, LRU_WIDTH), lambda i, t: (i, t, 0))
    par = lambda r, wd: pl.BlockSpec((r, wd), lambda i, t: (0, 0))
    nbytes = 2 * 2 * ts * LRU_WIDTH * 4 + 2 * ts * LRU_WIDTH * 2 + 2 * LRU_WIDTH * 2 * LRU_WIDTH * 2 \
        + 2 * ts * LRU_WIDTH * 4 + 8 * ts * LRU_WIDTH * 4
    return pl.pallas_call(
        _lru_kernel,
        grid=(b, s // ts),
        in_specs=[tok, tok, par(LRU_WIDTH, 2 * LRU_WIDTH), par(1, 2 * LRU_WIDTH), par(1, LRU_WIDTH)],
        out_specs=tok,
        out_shape=jax.ShapeDtypeStruct((b, s, LRU_WIDTH), BF16),
        scratch_shapes=[pltpu.VMEM((ts, LRU_WIDTH), F32),
                        pltpu.VMEM((ts, LRU_WIDTH), F32),
                        pltpu.VMEM((1, LRU_WIDTH), F32)],
        compiler_params=_params(("parallel", "arbitrary"), nbytes),
        name="lru",
    )(x_lru, z_lru, w_gate, b_gate, lam)


def _outproj_kernel(h_ref, ya_ref, ys_ref, yl_ref, w_ref, o_ref):
    a0, a1, a2 = ATT_WIDTH, ATT_WIDTH + SSD_WIDTH, MIX_WIDTH
    o_ref[...] = (h_ref[...]
                  + jnp.dot(ya_ref[...], w_ref[0:a0, :], preferred_element_type=F32)
                  + jnp.dot(ys_ref[...], w_ref[a0:a1, :], preferred_element_type=F32)
                  + jnp.dot(yl_ref[...], w_ref[a1:a2, :], preferred_element_type=F32))


def _outproj(h, y_att, y_ssd, y_lru, w):
    t = h.shape[0]
    tm = TM_OUTPROJ
    tok = lambda wd: pl.BlockSpec((tm, wd), lambda i: (i, 0))
    nbytes = 2 * (2 * tm * D_MODEL * 4 + tm * MIX_WIDTH * 2 + MIX_WIDTH * D_MODEL * 2)
    return pl.pallas_call(
        _outproj_kernel,
        grid=(t // tm,),
        in_specs=[tok(D_MODEL), tok(ATT_WIDTH), tok(SSD_WIDTH), tok(LRU_WIDTH),
                  pl.BlockSpec((MIX_WIDTH, D_MODEL), lambda i: (0, 0))],
        out_specs=tok(D_MODEL),
        out_shape=jax.ShapeDtypeStruct((t, D_MODEL), F32),
        compiler_params=_params(("parallel",), nbytes),
        name="outproj",
    )(h, y_att, y_ssd, y_lru, w)


def _pad_lanes(v, lane0=0, width=V7X_LANES):
    return jnp.pad(v, (lane0, width - lane0 - v.shape[0]))[None, :]


_IN_OFF = {name: sum(IN_SIZES[:i]) for i, name in enumerate(IN_NAMES)}


def _regroup_kernel(w_ref, o_ref):
    w = w_ref[...]
    out = 0
    for run in _RUNS[:-1]:
        width = sum(wd for _, wd in run)
        src = _IN_OFF[run[0][0]]
        o_ref[:, out:out + width] = w[:, src:src + width].astype(BF16)
        out += width
    f0, d0 = _IN_OFF["f_raw"], _IN_OFF["dt_raw"] - SSD_DT_LANE0
    lane = lax.broadcasted_iota(jnp.int32, (w.shape[0], V7X_LANES), 1)
    fdt = jnp.where(lane < SSD_DT_LANE0, w[:, f0:f0 + V7X_LANES],
                    jnp.where(lane < SSD_DT_LANE0 + SSD_HEADS, w[:, d0:d0 + V7X_LANES], 0.0))
    o_ref[:, out:] = fdt.astype(BF16)


def _regroup_w_in(w_in, layer):
    assert [n for run in _RUNS[:-1] for n, _ in run] == [n for n in IN_NAMES
                                                          if n not in ("f_raw", "dt_raw")]
    assert _IN_OFF["f_raw"] % V7X_LANES == 0 and (_IN_OFF["dt_raw"] - SSD_DT_LANE0) % V7X_LANES == 0
    assert IN_SIZES[IN_NAMES.index("f_raw")] == SSD_DT_LANE0
    d_in = w_in.shape[-1]
    tr = TR_REGROUP
    steps = D_MODEL // tr
    nbytes = 2 * tr * (d_in * 4 + D_IN_PAD * 2) + 2 * tr * d_in * 4
    return pl.pallas_call(
        _regroup_kernel,
        grid=(steps,),
        in_specs=[pl.BlockSpec((tr, d_in), lambda i: (layer * steps + i, 0))],
        out_specs=pl.BlockSpec((tr, D_IN_PAD), lambda i: (i, 0)),
        out_shape=jax.ShapeDtypeStruct((D_MODEL, D_IN_PAD), BF16),
        compiler_params=_params(("parallel",), nbytes),
        name="regroup",
    )(w_in.reshape(-1, d_in))


def _block_diag(w):
    nb, d, e = w.shape
    tiled = jnp.tile(w.reshape(nb * d, e), (1, nb))
    row = lax.broadcasted_iota(jnp.int32, tiled.shape, 0) // d
    col = lax.broadcasted_iota(jnp.int32, tiled.shape, 1) // e
    return jnp.where(row == col, tiled, 0.0)


def kernel(x, norm_g, w_in, q_norm_g, k_norm_g, forget_b, ssd_conv_w, ssd_conv_b, ssd_dt_bias,
           ssd_a_log, ssd_d, ssd_norm_g, lru_conv_w, lru_conv_b, lru_w_a, lru_b_a, lru_w_x, lru_b_x,
           lru_lambda, w_out):
    bsz, seq, d = x.shape
    assert d == D_MODEL and seq % max(TQ_ATT, TM_INPROJ, TS_LRU, SSD_CHUNKS_PER_STEP * SSD_CHUNK) == 0
    assert (bsz * seq) % TM_OUTPROJ == 0
    depth = w_in.shape[0]
    h = x.astype(F32).reshape(bsz * seq, D_MODEL)
    for l in range(depth):
        qt, ka, vt, z_att, z_ssd, xbc, x_lru, z_lru, dt_raw = _inproj(
            h.reshape(bsz, seq, D_MODEL), norm_g[l][None, :], _regroup_w_in(w_in, l),
            jnp.tile(q_norm_g[l], 2)[None, :], jnp.tile(k_norm_g[l], 2)[None, :],
            _pad_lanes(forget_b[l]), ssd_conv_w[l], ssd_conv_b[l][None, :],
            lru_conv_w[l], lru_conv_b[l][None, :])
        y_att = _attention(qt, ka, vt, z_att)
        y_ssd = _ssd(xbc, z_ssd, dt_raw, _pad_lanes(ssd_dt_bias[l], SSD_DT_LANE0),
                     _pad_lanes(ssd_a_log[l], SSD_DT_LANE0),
                     jnp.repeat(ssd_d[l], SSD_HEAD_DIM)[None, :], ssd_norm_g[l][None, :])
        b_gate = jnp.concatenate([lru_b_a[l], lru_b_x[l]])[None, :]
        w_gate = jnp.concatenate([_block_diag(lru_w_a[l]), _block_diag(lru_w_x[l])], axis=1).astype(BF16)
        y_lru = _lru(x_lru, z_lru, w_gate, b_gate, lru_lambda[l][None, :])
        h = _outproj(h, y_att.reshape(bsz * seq, ATT_WIDTH), y_ssd.reshape(bsz * seq, SSD_WIDTH),
                     y_lru.reshape(bsz * seq, LRU_WIDTH), w_out[l].astype(BF16))
    return h.reshape(bsz, seq, D_MODEL).astype(x.dtype)
```
